```python
import math
import jax, jax.numpy as jnp
from jax import lax
import numpy as np

D_MODEL = 2048
BATCH = 2
SEQ = 4096
DEPTH = 2
DEC_BATCH = 128
DEC_SEQ = 8
PAST_LEN = 2048
PAGE_SIZE = 128

D_PLE = 256
RMS_EPS = 1e-6
SB_HEADS = 8
SB_HEAD_DIM = 128
SB_WIDTH = SB_HEADS * SB_HEAD_DIM
SB_BLOCK = 128
LRU_WIDTH = D_MODEL // 4
LRU_BLOCKS = 8
LRU_BLOCK_DIM = LRU_WIDTH // LRU_BLOCKS
CONV_WIDTH = 4
LRU_C = 8.0
RW_WIDTH = D_MODEL // 4
RW_HEAD_DIM = 64
RW_HEADS = RW_WIDTH // RW_HEAD_DIM
RW_LORA_W = 64
RW_LORA_A = 64
RW_GN_EPS = 64e-5
RW_SHIFT_WIDTH = 3 * RW_WIDTH + RW_LORA_W + RW_LORA_A
OFF_SB_Q = 0
OFF_SB_K = OFF_SB_Q + SB_WIDTH
OFF_SB_V = OFF_SB_K + SB_WIDTH
OFF_SB_G = OFF_SB_V + SB_WIDTH
OFF_LRU_X = OFF_SB_G + SB_WIDTH
OFF_LRU_G = OFF_LRU_X + LRU_WIDTH
OFF_RW = OFF_LRU_G + LRU_WIDTH
OFF_RW_G = OFF_RW + RW_SHIFT_WIDTH
D_IN_PROJ = OFF_RW_G + RW_WIDTH
D_MIX = SB_WIDTH + LRU_WIDTH + RW_WIDTH

kernel_name = 'hybrid_sb_rglru_rwkv7_step'


def _rmsnorm(x, g):
    xf = x.astype(jnp.float32)
    y = xf * lax.rsqrt(jnp.mean(xf * xf, axis=-1, keepdims=True) + RMS_EPS)
    return (y * g.astype(jnp.float32)).astype(x.dtype)


def _stick_breaking(q, k, v, past_len):
    b, t, h, d = q.shape
    s = k.shape[1]
    qb = min(SB_BLOCK, t)
    nb = t // qb
    q_blocks = q.reshape(b, nb, qb, h, d).swapaxes(0, 1)
    pos_blocks = (past_len + jnp.arange(t, dtype=jnp.int32)).reshape(nb, qb)
    k_pos = jnp.arange(s, dtype=jnp.int32)
    scale = 1.0 / math.sqrt(d)

    def block(args):
        qblk, qpos = args
        z = jnp.einsum('bqhd,bshd->bhqs', qblk, k, preferred_element_type=jnp.float32) * scale
        mask = k_pos[None, :] < qpos[:, None]
        log_stay = jnp.where(mask, jax.nn.log_sigmoid(-z), 0.0)
        incl = jnp.cumsum(log_stay, axis=-1)
        log_w = jax.nn.log_sigmoid(z) + (incl[..., -1:] - incl)
        w = jnp.where(mask, jnp.exp(log_w), 0.0)
        return jnp.einsum('bhqs,bshd->bqhd', w.astype(v.dtype), v)

    out = lax.map(block, (q_blocks, pos_blocks))
    return out.swapaxes(0, 1).reshape(b, t, h * d)


def _rg_lru(xb, conv_buf, h0, lp):
    f32 = jnp.float32
    b, t, w = xb.shape
    xpad = jnp.concatenate([conv_buf.astype(xb.dtype), xb], axis=1)
    xc = lp['conv_b']
    for j in range(CONV_WIDTH):
        xc = xc + xpad[:, j:j + t] * lp['conv_w'][j]
    new_buf = xpad[:, t:]
    xh = xc.reshape(b, t, LRU_BLOCKS, LRU_BLOCK_DIM)
    r = jax.nn.sigmoid(jnp.einsum('bthi,hij->bthj', xh, lp['gate_a_w']).reshape(b, t, w) + lp['gate_a_b'])
    i = jax.nn.sigmoid(jnp.einsum('bthi,hij->bthj', xh, lp['gate_x_w']).reshape(b, t, w) + lp['gate_x_b'])
    log_a = -LRU_C * r.astype(f32) * jax.nn.softplus(-lp['lam'].astype(f32))
    a = jnp.exp(log_a)
    mult = jnp.sqrt(-jnp.expm1(2.0 * log_a))
    bterm = mult * (i * xc).astype(f32)
    bterm = bterm.at[:, 0].add(a[:, 0] * h0.astype(f32))

    def combine(e1, e2):
        a1, b1 = e1
        a2, b2 = e2
        return a1 * a2, a2 * b1 + b2

    _, h = lax.associative_scan(combine, (a, bterm), axis=1)
    return h.astype(xb.dtype), h[:, -1].astype(xb.dtype), new_buf


def _rwkv7(z, shift0, s0, lp):
    f32 = jnp.float32
    b, t, _ = z.shape
    W, H, N = RW_WIDTH, RW_HEADS, RW_HEAD_DIM
    z_prev = jnp.concatenate([shift0[:, None, :].astype(z.dtype), z[:, :-1]], axis=1)
    zm = (z + lp['rw_mu'] * (z_prev - z)).astype(f32)
    r = zm[..., :W]
    k = zm[..., W:2 * W]
    v = zm[..., 2 * W:3 * W]
    xw = zm[..., 3 * W:3 * W + RW_LORA_W]
    xa = zm[..., 3 * W + RW_LORA_W:]
    w_log = -jax.nn.softplus(-(lp['rw_w0'] + jnp.tanh(xw) @ lp['rw_w2'])) - 0.5
    decay = jnp.exp(-jnp.exp(w_log))
    a = jax.nn.sigmoid(lp['rw_a0'] + xa @ lp['rw_a2'])
    kk = (k * lp['rw_k_k']).reshape(b, t, H, N)
    kk = kk * lax.rsqrt(jnp.sum(kk * kk, axis=-1, keepdims=True) + 1e-12)
    k = k * (1.0 + (a - 1.0) * lp['rw_k_a'])

    def heads_t(arr):
        return jnp.moveaxis(arr.reshape(b, t, H, N), 1, 0)

    kk_t = jnp.moveaxis(kk, 1, 0)
    xs = (heads_t(r), heads_t(decay), heads_t(k), heads_t(v), -kk_t, kk_t * heads_t(a))

    def step(S, inp):
        r_t, w_t, k_t, v_t, aa_t, bb_t = inp
        sa = jnp.einsum('bhvk,bhk->bhv', S, aa_t)
        S = S * w_t[:, :, None, :] + sa[..., None] * bb_t[:, :, None, :] + v_t[..., None] * k_t[:, :, None, :]
        return S, jnp.einsum('bhvk,bhk->bhv', S, r_t)

    s_final, y = lax.scan(step, s0.astype(f32), xs)
    y = jnp.moveaxis(y, 0, 1)
    mean = jnp.mean(y, axis=-1, keepdims=True)
    var = jnp.mean((y - mean) ** 2, axis=-1, keepdims=True)
    y = ((y - mean) * lax.rsqrt(var + RW_GN_EPS)).reshape(b, t, W) * lp['rw_gn_w'] + lp['rw_gn_b']
    rh = r.reshape(b, t, H, N)
    kh = k.reshape(b, t, H, N)
    vh = v.reshape(b, t, H, N)
    bonus = jnp.sum(rh * kh * lp['rw_r_k'], axis=-1, keepdims=True) * vh
    y = y + bonus.reshape(b, t, W)
    return y.astype(z.dtype), s_final.astype(z.dtype), z[:, -1]


def _layer(x, p, k_past, v_past, conv_buf, h0, shift0, s0, lp):
    b, t, _ = x.shape
    past_len = k_past.shape[1]
    u = _rmsnorm(x, lp['norm_g'])
    proj = u @ lp['w_in']
    q = proj[..., OFF_SB_Q:OFF_SB_K].reshape(b, t, SB_HEADS, SB_HEAD_DIM)
    k_new = proj[..., OFF_SB_K:OFF_SB_V].reshape(b, t, SB_HEADS, SB_HEAD_DIM)
    v_new = proj[..., OFF_SB_V:OFF_SB_G].reshape(b, t, SB_HEADS, SB_HEAD_DIM)
    k_all = jnp.concatenate([k_past.astype(k_new.dtype), k_new], axis=1)
    v_all = jnp.concatenate([v_past.astype(v_new.dtype), v_new], axis=1)
    o_sb = _stick_breaking(q, k_all, v_all, past_len) * jax.nn.silu(proj[..., OFF_SB_G:OFF_LRU_X])
    o_lru, h_last, conv_new = _rg_lru(proj[..., OFF_LRU_X:OFF_LRU_G], conv_buf, h0, lp)
    o_lru = o_lru * jax.nn.silu(proj[..., OFF_LRU_G:OFF_RW])
    o_rw, s_new, shift_new = _rwkv7(proj[..., OFF_RW:OFF_RW_G], shift0, s0, lp)
    o_rw = o_rw * jax.nn.silu(proj[..., OFF_RW_G:D_IN_PROJ])
    h = x + jnp.concatenate([o_sb, o_lru, o_rw], axis=-1) @ lp['w_out']
    h = h + jax.nn.sigmoid(h @ lp['ple_gate']) * (p @ lp['ple_proj'])
    return h, (k_new, v_new, h_last, conv_new, s_new, shift_new)


def setup_inputs(seed: int = 0) -> dict:
    key = jax.random.key(seed)
    keys = iter(jax.random.split(key, 40))
    f32 = jnp.float32
    n_pages = PAST_LEN // PAGE_SIZE
    n_phys = (DEC_BATCH * n_pages * 5) // 4

    def normal(shape, scale):
        return scale * jax.random.normal(next(keys), shape, f32)

    x_prompt = normal((BATCH, SEQ, D_MODEL), 1.0)
    x_sample = normal((DEC_BATCH, DEC_SEQ, D_MODEL), 1.0)
    p_prompt = normal((DEPTH, BATCH, SEQ, D_PLE), 1.0)
    p_sample = normal((DEPTH, DEC_BATCH, DEC_SEQ, D_PLE), 1.0)
    cache_sb_k = normal((DEPTH, n_phys, PAGE_SIZE, SB_HEADS, SB_HEAD_DIM), 1.0)
    cache_sb_v = normal((DEPTH, n_phys, PAGE_SIZE, SB_HEADS, SB_HEAD_DIM), 1.0)
    page_table = jax.random.permutation(next(keys), n_phys)[:DEC_BATCH * n_pages].reshape(DEC_BATCH, n_pages).astype(jnp.int32)
    state_lru_h = normal((DEPTH, DEC_BATCH, LRU_WIDTH), 0.5)
    state_lru_conv = normal((DEPTH, DEC_BATCH, CONV_WIDTH - 1, LRU_WIDTH), 1.0)
    state_rw_S = normal((DEPTH, DEC_BATCH, RW_HEADS, RW_HEAD_DIM, RW_HEAD_DIM), 0.5)
    state_rw_shift = normal((DEPTH, DEC_BATCH, RW_SHIFT_WIDTH), 1.0)

    norm_g = 1.0 + normal((DEPTH, D_MODEL), 0.02)
    w_in = normal((DEPTH, D_MODEL, D_IN_PROJ), D_MODEL ** -0.5)
    w_out = normal((DEPTH, D_MIX, D_MODEL), D_MIX ** -0.5)
    lru_conv_w = normal((DEPTH, CONV_WIDTH, LRU_WIDTH), CONV_WIDTH ** -0.5)
    lru_conv_b = normal((DEPTH, LRU_WIDTH), 0.01)
    lru_gate_a_w = normal((DEPTH, LRU_BLOCKS, LRU_BLOCK_DIM, LRU_BLOCK_DIM), LRU_BLOCK_DIM ** -0.5)
    lru_gate_a_b = normal((DEPTH, LRU_WIDTH), 0.01)
    lru_gate_x_w = normal((DEPTH, LRU_BLOCKS, LRU_BLOCK_DIM, LRU_BLOCK_DIM), LRU_BLOCK_DIM ** -0.5)
    lru_gate_x_b = normal((DEPTH, LRU_WIDTH), 0.01)
    u = jax.random.uniform(next(keys), (DEPTH, LRU_WIDTH), f32, 0.9, 0.999)
    a_base = u ** (1.0 / LRU_C)
    lru_lambda = jnp.log(a_base) - jnp.log1p(-a_base)
    rw_mu = jax.random.uniform(next(keys), (DEPTH, RW_SHIFT_WIDTH), f32)
    rw_w0 = normal((DEPTH, RW_WIDTH), 0.5)
    rw_w2 = normal((DEPTH, RW_LORA_W, RW_WIDTH), 0.1 * RW_LORA_W ** -0.5)
    rw_a0 = normal((DEPTH, RW_WIDTH), 0.1)
    rw_a2 = normal((DEPTH, RW_LORA_A, RW_WIDTH), 0.1 * RW_LORA_A ** -0.5)
    rw_k_k = 0.85 + normal((DEPTH, RW_WIDTH), 0.02)
    rw_k_a = 1.0 + normal((DEPTH, RW_WIDTH), 0.02)
    rw_r_k = normal((DEPTH, RW_HEADS, RW_HEAD_DIM), 0.1)
    rw_gn_w = 1.0 + normal((DEPTH, RW_WIDTH), 0.02)
    rw_gn_b = normal((DEPTH, RW_WIDTH), 0.01)
    ple_proj = normal((DEPTH, D_PLE, D_MODEL), D_PLE ** -0.5)
    ple_gate = normal((DEPTH, D_MODEL, D_MODEL), D_MODEL ** -0.5)
    final_norm_g = 1.0 + normal((D_MODEL,), 0.02)
    return {'x_prompt': x_prompt, 'x_sample': x_sample, 'p_prompt': p_prompt, 'p_sample': p_sample,
            'cache_sb_k': cache_sb_k, 'cache_sb_v': cache_sb_v, 'page_table': page_table,
            'state_lru_h': state_lru_h, 'state_lru_conv': state_lru_conv,
            'state_rw_S': state_rw_S, 'state_rw_shift': state_rw_shift,
            'norm_g': norm_g, 'w_in': w_in, 'w_out': w_out,
            'lru_conv_w': lru_conv_w, 'lru_conv_b': lru_conv_b,
            'lru_gate_a_w': lru_gate_a_w, 'lru_gate_a_b': lru_gate_a_b,
            'lru_gate_x_w': lru_gate_x_w, 'lru_gate_x_b': lru_gate_x_b, 'lru_lambda': lru_lambda,
            'rw_mu': rw_mu, 'rw_w0': rw_w0, 'rw_w2': rw_w2, 'rw_a0': rw_a0, 'rw_a2': rw_a2,
            'rw_k_k': rw_k_k, 'rw_k_a': rw_k_a, 'rw_r_k': rw_r_k, 'rw_gn_w': rw_gn_w, 'rw_gn_b': rw_gn_b,
            'ple_proj': ple_proj, 'ple_gate': ple_gate, 'final_norm_g': final_norm_g}


def reference(x_prompt, x_sample, p_prompt, p_sample, cache_sb_k, cache_sb_v, page_table,
              state_lru_h, state_lru_conv, state_rw_S, state_rw_shift,
              norm_g, w_in, w_out, lru_conv_w, lru_conv_b, lru_gate_a_w, lru_gate_a_b,
              lru_gate_x_w, lru_gate_x_b, lru_lambda, rw_mu, rw_w0, rw_w2, rw_a0, rw_a2,
              rw_k_k, rw_k_a, rw_r_k, rw_gn_w, rw_gn_b, ple_proj, ple_gate, final_norm_g):
    dt = x_prompt.dtype
    bp = x_prompt.shape[0]
    bs = x_sample.shape[0]
    hp, hs = x_prompt, x_sample
    outs_p = []
    outs_s = []
    for l in range(DEPTH):
        lp = dict(norm_g=norm_g[l], w_in=w_in[l], w_out=w_out[l],
                  conv_w=lru_conv_w[l], conv_b=lru_conv_b[l],
                  gate_a_w=lru_gate_a_w[l], gate_a_b=lru_gate_a_b[l],
                  gate_x_w=lru_gate_x_w[l], gate_x_b=lru_gate_x_b[l], lam=lru_lambda[l],
                  rw_mu=rw_mu[l], rw_w0=rw_w0[l], rw_w2=rw_w2[l], rw_a0=rw_a0[l], rw_a2=rw_a2[l],
                  rw_k_k=rw_k_k[l], rw_k_a=rw_k_a[l], rw_r_k=rw_r_k[l],
                  rw_gn_w=rw_gn_w[l], rw_gn_b=rw_gn_b[l],
                  ple_proj=ple_proj[l], ple_gate=ple_gate[l])
        empty_kv = jnp.zeros((bp, 0, SB_HEADS, SB_HEAD_DIM), dt)
        hp, st_p = _layer(hp, p_prompt[l], empty_kv, empty_kv,
                          jnp.zeros((bp, CONV_WIDTH - 1, LRU_WIDTH), dt),
                          jnp.zeros((bp, LRU_WIDTH), dt),
                          jnp.zeros((bp, RW_SHIFT_WIDTH), dt),
                          jnp.zeros((bp, RW_HEADS, RW_HEAD_DIM, RW_HEAD_DIM), dt), lp)
        outs_p.append(st_p)
        k_past = cache_sb_k[l][page_table].reshape(bs, -1, SB_HEADS, SB_HEAD_DIM)
        v_past = cache_sb_v[l][page_table].reshape(bs, -1, SB_HEADS, SB_HEAD_DIM)
        hs, st_s = _layer(hs, p_sample[l], k_past, v_past, state_lru_conv[l], state_lru_h[l],
                          state_rw_shift[l], state_rw_S[l], lp)
        outs_s.append(st_s)
    y_prompt = _rmsnorm(hp, final_norm_g)
    y_sample = _rmsnorm(hs, final_norm_g)
    sb_k_prompt = jnp.stack([o[0] for o in outs_p])
    sb_v_prompt = jnp.stack([o[1] for o in outs_p])
    sb_k_sample = jnp.stack([o[0] for o in outs_s])
    sb_v_sample = jnp.stack([o[1] for o in outs_s])
    lru_h_prompt = jnp.stack([o[2] for o in outs_p])
    lru_h_sample = jnp.stack([o[2] for o in outs_s])
    lru_conv_prompt = jnp.stack([o[3] for o in outs_p])
    lru_conv_sample = jnp.stack([o[3] for o in outs_s])
    rw_S_prompt = jnp.stack([o[4] for o in outs_p])
    rw_S_sample = jnp.stack([o[4] for o in outs_s])
    rw_shift_prompt = jnp.stack([o[5] for o in outs_p])
    rw_shift_sample = jnp.stack([o[5] for o in outs_s])
    return (y_prompt, y_sample, sb_k_prompt, sb_v_prompt, sb_k_sample, sb_v_sample,
            lru_h_prompt, lru_h_sample, lru_conv_prompt, lru_conv_sample,
            rw_S_prompt, rw_S_sample, rw_shift_prompt, rw_shift_sample)
```

```python
import functools
import math

import jax
import jax.numpy as jnp
from jax import lax
from jax.experimental import pallas as pl
from jax.experimental.pallas import tpu as pltpu

F32 = jnp.float32
BF16 = jnp.bfloat16

D_MODEL = 2048
D_PLE = 256
RMS_EPS = 1e-6
SB_HEADS = 8
SB_HEAD_DIM = 128
SB_WIDTH = SB_HEADS * SB_HEAD_DIM
LRU_WIDTH = 512
LRU_BLOCKS = 8
CONV_WIDTH = 4
LRU_C = 8.0
RW_WIDTH = 512
RW_HEAD_DIM = 64
RW_HEADS = 8
RW_LORA = 64
RW_GN_EPS = 64e-5
RW_SHIFT_WIDTH = 3 * RW_WIDTH + 2 * RW_LORA
OFF_SB_Q = 0
OFF_SB_K = OFF_SB_Q + SB_WIDTH
OFF_SB_V = OFF_SB_K + SB_WIDTH
OFF_SB_G = OFF_SB_V + SB_WIDTH
OFF_LRU_X = OFF_SB_G + SB_WIDTH
OFF_LRU_G = OFF_LRU_X + LRU_WIDTH
OFF_RW = OFF_LRU_G + LRU_WIDTH
OFF_RW_XWA = OFF_RW + 3 * RW_WIDTH
OFF_RW_G = OFF_RW + RW_SHIFT_WIDTH
D_IN_PROJ = OFF_RW_G + RW_WIDTH

LANES = 128
SUBLANES = 8
VMEM_LIMIT = 56 * 1024 * 1024

LOG_ZERO = -110.0


def _cparams(sem, vmem=VMEM_LIMIT):
    return pltpu.CompilerParams(dimension_semantics=sem, vmem_limit_bytes=vmem)


def _dot(a, b):
    return jnp.dot(a, b, preferred_element_type=F32)


def _dot_nt(a, b):
    return lax.dot_general(a, b, (((1,), (1,)), ((), ())), preferred_element_type=F32)


def _dot_tn(a, b):
    return lax.dot_general(a, b, (((0,), (0,)), ((), ())), preferred_element_type=F32)


def _split3(x):
    x1 = x.astype(BF16)
    r1 = x - x1.astype(F32)
    x2 = r1.astype(BF16)
    x3 = (r1 - x2.astype(F32)).astype(BF16)
    return x1, x2, x3


def _dot_exact_lhs(x, m):
    x1, x2, x3 = _split3(x)
    return _dot(x1, m) + _dot(x2, m) + _dot(x3, m)


def _dot_exact_rhs(m, x):
    x1, x2, x3 = _split3(x)
    return _dot(m, x1) + _dot(m, x2) + _dot(m, x3)


def _iota(shape, dim):
    return lax.broadcasted_iota(jnp.int32, shape, dim)


def _silu(x):
    return x * jax.nn.sigmoid(x)


def _softplus(x):
    return jnp.maximum(x, 0.0) + jnp.log1p(jnp.exp(-jnp.abs(x)))


EXPM1_SERIES_TERMS = 9
EXPM1_SERIES_RANGE = 0.25


def _neg_expm1(x, ex):
    p = 1.0 + x * (1.0 / EXPM1_SERIES_TERMS)
    for n in range(EXPM1_SERIES_TERMS - 1, 1, -1):
        p = 1.0 + (x * (1.0 / n)) * p
    return jnp.where(x > -EXPM1_SERIES_RANGE, -x * p, 1.0 - ex)


def _log_sigmoid_pair(z):
    t = jnp.log1p(jnp.exp(-jnp.abs(z)))
    return jnp.minimum(z, 0.0) - t, -jnp.maximum(z, 0.0) - t


def _inproj_body(x_ref, g_ref, w_ref, o_ref, u_ref):
    @pl.when(pl.program_id(1) == 0)
    def _():
        x = x_ref[...]
        ms = jnp.mean(x * x, axis=-1, keepdims=True)
        u_ref[...] = (x * lax.rsqrt(ms + RMS_EPS) * g_ref[...]).astype(BF16)

    o_ref[...] = _dot(u_ref[...], w_ref[...])


def _inproj(x, g, w, *, tm, tn):
    n, d = x.shape
    dout = w.shape[1]
    return pl.pallas_call(
        _inproj_body,
        grid=(n // tm, dout // tn),
        in_specs=[pl.BlockSpec((tm, d), lambda i, j: (i, 0)),
                  pl.BlockSpec((1, d), lambda i, j: (0, 0)),
                  pl.BlockSpec((d, tn), lambda i, j: (0, j))],
        out_specs=pl.BlockSpec((tm, tn), lambda i, j: (i, j)),
        out_shape=jax.ShapeDtypeStruct((n, dout), F32),
        scratch_shapes=[pltpu.VMEM((tm, d), BF16)],
        compiler_params=_cparams(("parallel", "arbitrary")),
        name="inproj",
    )(x, g, w)


def _outproj_body(x_ref, osb_ref, olru_ref, orw_ref, p_ref, w1_ref, w2_ref, w3_ref,
                  wg_ref, wp_ref, fg_ref, o_ref, *, final):
    h = x_ref[...] + _dot(osb_ref[...], w1_ref[...]) + _dot(olru_ref[...], w2_ref[...]) \
        + _dot(orw_ref[...], w3_ref[...])
    gate = _dot(h.astype(BF16), wg_ref[...])
    pp = _dot(p_ref[...].astype(BF16), wp_ref[...])
    h = h + jax.nn.sigmoid(gate) * pp
    if final:
        ms = jnp.mean(h * h, axis=-1, keepdims=True)
        h = h * lax.rsqrt(ms + RMS_EPS) * fg_ref[...]
    o_ref[...] = h


def _outproj(x, osb, olru, orw, p, w_out, w_gate, w_ple, fg, *, tm, final):
    n, d = x.shape
    wsb, wlru, wrw = osb.shape[1], olru.shape[1], orw.shape[1]
    dp = p.shape[1]
    row = lambda w: pl.BlockSpec((tm, w), lambda i: (i, 0))
    const = lambda shape, idx: pl.BlockSpec(shape, lambda i: idx)
    return pl.pallas_call(
        functools.partial(_outproj_body, final=final),
        grid=(n // tm,),
        in_specs=[row(d), row(wsb), row(wlru), row(wrw), row(dp),
                  const((wsb, d), (0, 0)),
                  const((wlru, d), (wsb // wlru, 0)),
                  const((wrw, d), ((wsb + wlru) // wrw, 0)),
                  const((d, d), (0, 0)),
                  const((dp, d), (0, 0)),
                  const((1, d), (0, 0))],
        out_specs=row(d),
        out_shape=jax.ShapeDtypeStruct((n, d), F32),
        compiler_params=_cparams(("parallel",)),
        name="outproj",
    )(x, osb, olru, orw, p, w_out, w_out, w_out, w_gate, w_ple, fg)


def _attn_prompt_body(q_ref, k_ref, v_ref, g_ref, o_ref, acc_ref, r_ref, *, tq, scale):
    qi = pl.program_id(2)
    q = q_ref[...].astype(BF16)
    acc_ref[...] = jnp.zeros_like(acc_ref)
    r_ref[...] = jnp.zeros_like(r_ref)
    row = _iota((tq, tq), 0)
    col = _iota((tq, tq), 1)
    tri = (row > col).astype(BF16)
    diag_mask = col < row

    def cond(c):
        j, done = c
        return jnp.logical_and(j >= 0, done == 0)

    def body(c):
        j, _ = c
        start = pl.multiple_of(j * tq, tq)
        kb = k_ref[pl.ds(start, tq), :].astype(BF16)
        vb = v_ref[pl.ds(start, tq), :].astype(BF16)
        z = _dot_nt(q, kb) * scale
        mask = jnp.logical_or(j < qi, diag_mask)
        lsz, lstay = _log_sigmoid_pair(z)
        lstay = jnp.where(mask, lstay, 0.0)
        s_excl = _dot_exact_lhs(lstay, tri)
        r_old = r_ref[...]
        w = jnp.where(mask, jnp.exp(lsz + s_excl + r_old), 0.0)
        acc_ref[...] += _dot(w.astype(BF16), vb)
        r_new = r_old + s_excl[:, 0:1] + lstay[:, 0:1]
        r_ref[...] = r_new
        done = (jnp.max(r_new) < LOG_ZERO).astype(jnp.int32)
        return j - 1, done

    lax.while_loop(cond, body, (qi, jnp.int32(0)))
    o_ref[...] = (acc_ref[...] * _silu(g_ref[...])).astype(o_ref.dtype)


def _attn_prompt(proj, *, batch, seq, tq):
    n_rows = batch * seq
    nq = seq // tq
    hd = SB_HEAD_DIM
    scale = 1.0 / math.sqrt(hd)
    qspec = lambda off: pl.BlockSpec((tq, hd), lambda b, h, i: (b * nq + i, off // hd + h))
    kvspec = lambda off: pl.BlockSpec((seq, hd), lambda b, h, i: (b, off // hd + h))
    return pl.pallas_call(
        functools.partial(_attn_prompt_body, tq=tq, scale=scale),
        grid=(batch, SB_HEADS, nq),
        in_specs=[qspec(OFF_SB_Q), kvspec(OFF_SB_K), kvspec(OFF_SB_V), qspec(OFF_SB_G)],
        out_specs=pl.BlockSpec((tq, hd), lambda b, h, i: (b * nq + i, h)),
        out_shape=jax.ShapeDtypeStruct((n_rows, SB_WIDTH), BF16),
        scratch_shapes=[pltpu.VMEM((tq, hd), F32), pltpu.VMEM((tq, 1), F32)],
        compiler_params=_cparams(("parallel", "parallel", "arbitrary")),
        name="attn_prompt",
    )(proj, proj, proj, proj)


def _attn_sample_body(pt_ref, q_ref, kn_ref, vn_ref, g_ref, kc_ref, vc_ref, o_ref,
                      kbuf, vbuf, sem, acc_ref, r_ref, of_ref, *, bb, t_new, n_pages, page, scale):
    i = pl.program_id(0)
    hq = SB_HEADS * t_new
    hd = SB_HEAD_DIM
    extra_slot = 2

    def page_copies(b, j, slot):
        pg = pt_ref[b, j]
        return (pltpu.make_async_copy(kc_ref.at[pg], kbuf.at[slot], sem.at[0, slot]),
                pltpu.make_async_copy(vc_ref.at[pg], vbuf.at[slot], sem.at[1, slot]))

    def start(b, j, slot):
        for c in page_copies(b, j, slot):
            c.start()

    def wait(b, j, slot):
        for c in page_copies(b, j, slot):
            c.wait()

    def block_update(kb, vb, qbd_t, tri, mask):
        zt = _dot_nt(kb.astype(BF16), qbd_t) * scale
        lsz, lstay = _log_sigmoid_pair(zt)
        if mask is not None:
            lstay = jnp.where(mask, lstay, 0.0)
        s_excl = _dot_exact_rhs(tri, lstay)
        r_old = r_ref[...]
        w = jnp.exp(lsz + s_excl + r_old)
        if mask is not None:
            w = jnp.where(mask, w, 0.0)
        acc_ref[...] += _dot_tn(w.astype(BF16), vb.astype(BF16))
        r_new = r_old + s_excl[0:1, :] + lstay[0:1, :]
        r_ref[...] = r_new
        return r_new

    tri_new = (_iota((t_new, t_new), 1) > _iota((t_new, t_new), 0)).astype(BF16)
    tri_page = (_iota((page, page), 1) > _iota((page, page), 0)).astype(BF16)
    new_mask = _iota((t_new, hq), 0) < (_iota((t_new, hq), 1) % t_new)
    head_mask = (_iota((hq, SB_WIDTH), 0) // t_new) == (_iota((hq, SB_WIDTH), 1) // hd)

    start(i * bb, n_pages - 1, 0)
    for s in range(bb):
        b = i * bb + s
        slot = s % 2
        if s + 1 < bb:
            start(b + 1, n_pages - 1, (s + 1) % 2)
        rows = slice(s * t_new, (s + 1) * t_new)
        q_b = q_ref[rows, :].astype(BF16)
        qbd_t = jnp.where(head_mask, jnp.concatenate([q_b] * SB_HEADS, axis=0), jnp.zeros((), BF16))
        acc_ref[...] = jnp.zeros_like(acc_ref)
        r_ref[...] = jnp.zeros_like(r_ref)
        block_update(kn_ref[rows, :], vn_ref[rows, :], qbd_t, tri_new, new_mask)
        wait(b, n_pages - 1, slot)
        r_new = block_update(kbuf[slot], vbuf[slot], qbd_t, tri_page, None)
        done0 = (jnp.max(r_new) < LOG_ZERO).astype(jnp.int32)

        def cond(c):
            j, done = c
            return jnp.logical_and(j >= 0, done == 0)

        def body(c, b=b, qbd_t=qbd_t):
            j, _ = c
            start(b, j, extra_slot)
            wait(b, j, extra_slot)
            r_j = block_update(kbuf[extra_slot], vbuf[extra_slot], qbd_t, tri_page, None)
            return j - 1, (jnp.max(r_j) < LOG_ZERO).astype(jnp.int32)

        lax.while_loop(cond, body, (jnp.int32(n_pages - 2), done0))
        acc = acc_ref[...]
        o_b = jnp.concatenate(
            [acc[h * t_new:(h + 1) * t_new, h * hd:(h + 1) * hd] for h in range(SB_HEADS)], axis=1)
        of_ref[rows, :] = o_b * _silu(g_ref[rows, :])
    o_ref[...] = of_ref[...].astype(o_ref.dtype)


def _attn_sample(proj, cache_k, cache_v, page_table, *, dec_batch, t_new, bb):
    n_phys, page = cache_k.shape[0], cache_k.shape[1]
    n_pages = page_table.shape[1]
    kc = cache_k.reshape(n_phys, page, SB_WIDTH)
    vc = cache_v.reshape(n_phys, page, SB_WIDTH)
    tb = bb * t_new
    spec = lambda off: pl.BlockSpec((tb, SB_WIDTH), lambda i, pt: (i, off // SB_WIDTH))
    hq = SB_HEADS * t_new
    grid_spec = pltpu.PrefetchScalarGridSpec(
        num_scalar_prefetch=1,
        grid=(dec_batch // bb,),
        in_specs=[spec(OFF_SB_Q), spec(OFF_SB_K), spec(OFF_SB_V), spec(OFF_SB_G),
                  pl.BlockSpec(memory_space=pl.ANY), pl.BlockSpec(memory_space=pl.ANY)],
        out_specs=pl.BlockSpec((tb, SB_WIDTH), lambda i, pt: (i, 0)),
        scratch_shapes=[pltpu.VMEM((3, page, SB_WIDTH), F32), pltpu.VMEM((3, page, SB_WIDTH), F32),
                        pltpu.SemaphoreType.DMA((2, 3)),
                        pltpu.VMEM((hq, SB_WIDTH), F32), pltpu.VMEM((1, hq), F32),
                        pltpu.VMEM((tb, SB_WIDTH), F32)],
    )
    return pl.pallas_call(
        functools.partial(_attn_sample_body, bb=bb, t_new=t_new, n_pages=n_pages, page=page,
                          scale=1.0 / math.sqrt(SB_HEAD_DIM)),
        grid_spec=grid_spec,
        out_shape=jax.ShapeDtypeStruct((dec_batch * t_new, SB_WIDTH), BF16),
        compiler_params=_cparams(("arbitrary",)),
        name="attn_sample",
    )(page_table, proj, proj, proj, proj, kc, vc)


def _lru_gates(xc, gw_ref, ga_b_ref, gx_b_ref, lam_ref):
    w = xc.shape[-1]
    gates = _dot(xc.astype(BF16), gw_ref[...])
    r = jax.nn.sigmoid(gates[:, :w] + ga_b_ref[...])
    i = jax.nn.sigmoid(gates[:, w:] + gx_b_ref[...])
    log_a = -LRU_C * r * _softplus(-lam_ref[...])
    a = jnp.exp(log_a)
    mult = jnp.sqrt(_neg_expm1(2.0 * log_a, a * a))
    return a, mult * (i * xc)


def _lru_prompt_body(x_ref, g_ref, cw_ref, cb_ref, gw_ref, ga_b_ref, gx_b_ref, lam_ref,
                     o_ref, hl_ref, xp_ref, a_ref, b_ref, h_ref, *, tb):
    c = pl.program_id(1)
    pad = SUBLANES
    hist = CONV_WIDTH - 1

    @pl.when(c == 0)
    def _():
        xp_ref[0:pad, :] = jnp.zeros((pad, LRU_WIDTH), F32)
        h_ref[...] = jnp.zeros_like(h_ref)

    xb = x_ref[...]
    xp_ref[pad:pad + tb, :] = xb
    xc = cb_ref[...] + xp_ref[pad - hist:pad - hist + tb, :] * cw_ref[0:1, :]
    for j in range(1, CONV_WIDTH):
        xc = xc + xp_ref[pad - hist + j:pad - hist + j + tb, :] * cw_ref[j:j + 1, :]
    xp_ref[pad - hist:pad, :] = xb[tb - hist:tb, :]

    a, bt = _lru_gates(xc, gw_ref, ga_b_ref, gx_b_ref, lam_ref)
    rowg = _iota((tb, LRU_WIDTH), 0) % SUBLANES
    s = 1
    while s < SUBLANES:
        a_sh = pltpu.roll(a, s, 0)
        b_sh = pltpu.roll(bt, s, 0)
        ok = rowg >= s
        bt = jnp.where(ok, a * b_sh + bt, bt)
        a = jnp.where(ok, a * a_sh, a)
        s *= 2
    a_ref[...] = a
    b_ref[...] = bt

    def group(gi, h):
        r0 = pl.multiple_of(gi * SUBLANES, SUBLANES)
        hs = a_ref[pl.ds(r0, SUBLANES), :] * h + b_ref[pl.ds(r0, SUBLANES), :]
        b_ref[pl.ds(r0, SUBLANES), :] = hs
        return jnp.broadcast_to(hs[SUBLANES - 1:SUBLANES, :], (SUBLANES, LRU_WIDTH))

    h = lax.fori_loop(0, tb // SUBLANES, group, h_ref[...])
    h_ref[...] = h
    hl_ref[0] = h[0:1, :]
    o_ref[...] = (b_ref[...] * _silu(g_ref[...])).astype(o_ref.dtype)


def _lru_prompt(proj, lw, *, batch, seq, tb):
    n_rows = batch * seq
    nc = seq // tb
    w = LRU_WIDTH
    const = lambda shape: pl.BlockSpec(shape, lambda b, c: (0,) * len(shape))
    return pl.pallas_call(
        functools.partial(_lru_prompt_body, tb=tb),
        grid=(batch, nc),
        in_specs=[pl.BlockSpec((tb, w), lambda b, c: (b * nc + c, OFF_LRU_X // w)),
                  pl.BlockSpec((tb, w), lambda b, c: (b * nc + c, OFF_LRU_G // w)),
                  const((CONV_WIDTH, w)), const((1, w)), const((w, 2 * w)),
                  const((1, w)), const((1, w)), const((1, w))],
        out_specs=[pl.BlockSpec((tb, w), lambda b, c: (b * nc + c, 0)),
                   pl.BlockSpec((1, 1, w), lambda b, c: (b, 0, 0))],
        out_shape=[jax.ShapeDtypeStruct((n_rows, w), BF16),
                   jax.ShapeDtypeStruct((batch, 1, w), F32)],
        scratch_shapes=[pltpu.VMEM((SUBLANES + tb, w), F32), pltpu.VMEM((tb, w), F32),
                        pltpu.VMEM((tb, w), F32), pltpu.VMEM((SUBLANES, w), F32)],
        compiler_params=_cparams(("arbitrary", "arbitrary")),
        name="lru_prompt",
    )(proj, proj, lw["conv_w"], lw["conv_b"], lw["gate_w"], lw["gate_a_b"], lw["gate_x_b"], lw["lam"])


def _lru_sample_body(*refs, nb, t_new):
    ncb = LRU_WIDTH // LANES
    x_refs, cbuf_refs = refs[:ncb], refs[ncb:2 * ncb]
    (g_ref, h0_ref, cw_ref, cb_ref, gw_ref, ga_b_ref, gx_b_ref, lam_ref,
     o_ref, hl_ref, of_ref) = refs[2 * ncb:]
    hist = CONV_WIDTH - 1
    gather = lambda rs, start, stride: jnp.concatenate(
        [r[pl.ds(start, nb, stride=stride), :] for r in rs], axis=1)
    xs = [gather(cbuf_refs, j, hist) for j in range(hist)]
    xs += [gather(x_refs, t, t_new) for t in range(t_new)]
    h = h0_ref[...]
    for t in range(t_new):
        xc = cb_ref[...]
        for j in range(CONV_WIDTH):
            xc = xc + xs[t + j] * cw_ref[j:j + 1, :]
        a, bt = _lru_gates(xc, gw_ref, ga_b_ref, gx_b_ref, lam_ref)
        h = a * h + bt
        for cb in range(of_ref.shape[0]):
            of_ref[cb, pl.ds(t, nb, stride=t_new), :] = h[:, cb * LANES:(cb + 1) * LANES]
    hl_ref[...] = h
    hs = jnp.concatenate([of_ref[cb] for cb in range(of_ref.shape[0])], axis=1)
    o_ref[...] = (hs * _silu(g_ref[...])).astype(o_ref.dtype)


def _lru_sample(proj, conv_buf, h0, lw, *, nb, t_new):
    w = LRU_WIDTH
    tb = nb * t_new
    const = lambda shape: pl.BlockSpec(shape, lambda i: (0,) * len(shape))
    ncb = w // LANES
    hist = CONV_WIDTH - 1
    colblk = lambda rows, c: pl.BlockSpec((rows, LANES), lambda i: (0, c))
    return pl.pallas_call(
        functools.partial(_lru_sample_body, nb=nb, t_new=t_new),
        grid=(1,),
        in_specs=[colblk(tb, OFF_LRU_X // LANES + q) for q in range(ncb)]
        + [colblk(nb * hist, q) for q in range(ncb)]
        + [pl.BlockSpec((tb, w), lambda i: (0, OFF_LRU_G // w)), const((nb, w)),
           const((CONV_WIDTH, w)), const((1, w)), const((w, 2 * w)),
           const((1, w)), const((1, w)), const((1, w))],
        out_specs=[const((tb, w)), const((nb, w))],
        out_shape=[jax.ShapeDtypeStruct((tb, w), BF16),
                   jax.ShapeDtypeStruct((nb, w), F32)],
        scratch_shapes=[pltpu.VMEM((ncb, tb, LANES), F32)],
        compiler_params=_cparams(("arbitrary",)),
        name="lru_sample",
    )(*([proj] * ncb), *([conv_buf.reshape(nb * hist, w)] * ncb), proj, h0,
      lw["conv_w"], lw["conv_b"], lw["gate_w"], lw["gate_a_b"], lw["gate_x_b"], lw["lam"])


def _seg_sum(x, seg):
    return _dot_exact_lhs(x, seg)


def _rw_prepare(z_r, z_k, z_v, z_x, p_r, p_k, p_v, p_x, rp):
    mu = rp["mu"]
    w = RW_WIDTH
    mix = lambda z, p, lo, hi: z + mu[:, lo:hi] * (p - z)
    r = mix(z_r, p_r, 0, w)
    k = mix(z_k, p_k, w, 2 * w)
    v = mix(z_v, p_v, 2 * w, 3 * w)
    x = mix(z_x, p_x, 3 * w, 3 * w + 2 * RW_LORA)
    seg = rp["seg"]
    w_log = -_softplus(-(rp["w0"] + _dot(jnp.tanh(x).astype(BF16), rp["w2"]))) - 0.5
    logw = -jnp.exp(w_log)
    a = jax.nn.sigmoid(rp["a0"] + _dot(x.astype(BF16), rp["a2"]))
    kk = k * rp["k_k"]
    kk = kk * lax.rsqrt(_seg_sum(kk * kk, seg) + 1e-12)
    k2 = k * (1.0 + (a - 1.0) * rp["k_a"])
    bonus = _seg_sum(r * k2 * rp["r_k"], seg) * v
    return r, logw, k2, v, -kk, kk * a, bonus


def _rw_params(refs):
    names = ("mu", "w0", "w2", "a0", "a2", "k_k", "k_a", "r_k", "seg")
    return {n: r[...] for n, r in zip(names, refs)}


def _rw_param_specs(idx):
    w = RW_WIDTH
    shapes = [(1, RW_SHIFT_WIDTH), (1, w), (2 * RW_LORA, w), (1, w), (2 * RW_LORA, w),
              (1, w), (1, w), (1, w), (w, w)]
    return [pl.BlockSpec(s, idx) for s in shapes]


def _rw_param_args(rw):
    return [rw[n] for n in ("mu", "w0", "w2", "a0", "a2", "k_k", "k_a", "r_k", "seg")]


def _rw_pre_prompt_body(zr_ref, zk_ref, zv_ref, zx_ref, *rest, tb):
    prm = _rw_params(rest[:9])
    outs = rest[9:16]
    last = rest[16:20]
    c = pl.program_id(1)

    @pl.when(c == 0)
    def _():
        for l in last:
            l[...] = jnp.zeros_like(l)

    zs = [zr_ref[...], zk_ref[...], zv_ref[...], zx_ref[...]]
    prevs = []
    for z, l in zip(zs, last):
        first = _iota(z.shape, 0) == 0
        prevs.append(jnp.where(first, jnp.broadcast_to(l[0:1, :], z.shape), pltpu.roll(z, 1, 0)))
        l[...] = jnp.broadcast_to(z[tb - 1:tb, :], l.shape)
    res = _rw_prepare(*zs, *prevs, prm)
    for o, val in zip(outs, res):
        o[...] = val


def _rw_pre_specs(row_idx):
    w = RW_WIDTH
    xw = 2 * RW_LORA
    return [(w, OFF_RW // w), (w, OFF_RW // w + 1), (w, OFF_RW // w + 2), (xw, OFF_RW_XWA // xw)]


def _rw_pre_prompt(proj, rw, *, batch, seq, tb):
    n_rows = batch * seq
    nc = seq // tb
    w = RW_WIDTH
    cols = _rw_pre_specs(None)
    in_specs = [pl.BlockSpec((tb, cw), functools.partial(lambda b, c, cb: (b * nc + c, cb), cb=cb))
                for cw, cb in cols]
    in_specs += _rw_param_specs(lambda b, c: (0, 0))
    out_spec = pl.BlockSpec((tb, w), lambda b, c: (b * nc + c, 0))
    return pl.pallas_call(
        functools.partial(_rw_pre_prompt_body, tb=tb),
        grid=(batch, nc),
        in_specs=in_specs,
        out_specs=[out_spec] * 7,
        out_shape=[jax.ShapeDtypeStruct((n_rows, w), F32)] * 7,
        scratch_shapes=[pltpu.VMEM((SUBLANES, cw), F32) for cw, _ in cols],
        compiler_params=_cparams(("arbitrary", "arbitrary")),
        name="rw_pre_prompt",
    )(proj, proj, proj, proj, *_rw_param_args(rw))


def _rw_pre_sample_body(zr_ref, zk_ref, zv_ref, zx_ref, sr_ref, sk_ref, sv_ref, sx_ref, *rest,
                        nb, t_new):
    prm = _rw_params(rest[:9])
    outs = rest[9:16]
    spread_refs = rest[16:20]
    zs = [zr_ref[...], zk_ref[...], zv_ref[...], zx_ref[...]]
    prevs = []
    for z, s_ref, e_ref in zip(zs, (sr_ref, sk_ref, sv_ref, sx_ref), spread_refs):
        e_ref[...] = jnp.zeros_like(e_ref)
        for cb in range(e_ref.shape[0]):
            e_ref[cb, pl.ds(0, nb, stride=t_new), :] = s_ref[:, cb * LANES:(cb + 1) * LANES]
        spread = jnp.concatenate([e_ref[cb] for cb in range(e_ref.shape[0])], axis=1)
        first = (_iota(z.shape, 0) % t_new) == 0
        prevs.append(jnp.where(first, spread, pltpu.roll(z, 1, 0)))
    res = _rw_prepare(*zs, *prevs, prm)
    for o, val in zip(outs, res):
        o[...] = val


def _rw_pre_sample(proj, shift0, rw, *, nb, t_new):
    w = RW_WIDTH
    tb = nb * t_new
    cols = _rw_pre_specs(None)
    in_specs = [pl.BlockSpec((tb, cw), functools.partial(lambda i, cb: (0, cb), cb=cb)) for cw, cb in cols]
    shifts = [shift0[:, 0:w], shift0[:, w:2 * w], shift0[:, 2 * w:3 * w], shift0[:, 3 * w:]]
    in_specs += [pl.BlockSpec(s.shape, lambda i: (0, 0)) for s in shifts]
    in_specs += _rw_param_specs(lambda i: (0, 0))
    out_spec = pl.BlockSpec((tb, w), lambda i: (0, 0))
    return pl.pallas_call(
        functools.partial(_rw_pre_sample_body, nb=nb, t_new=t_new),
        grid=(1,),
        in_specs=in_specs,
        out_specs=[out_spec] * 7,
        out_shape=[jax.ShapeDtypeStruct((tb, w), F32)] * 7,
        scratch_shapes=[pltpu.VMEM((cw // LANES, tb, LANES), F32) for cw, _ in cols],
        compiler_params=_cparams(("arbitrary",)),
        name="rw_pre_sample",
    )(proj, proj, proj, proj, *shifts, *_rw_param_args(rw))


RW_CHUNK = 64


def _rw_chunk_head(r, lw_incl, lw_excl, l_tot, k, v, aa, bb, s0, masks):
    strict, incl, eye = masks
    c = r.shape[0]
    at = (aa * jnp.exp(lw_excl)).astype(BF16)
    rt = r * jnp.exp(lw_incl)
    en = jnp.exp(-lw_incl)
    kh = (k * en).astype(BF16)
    bh = (bb * en).astype(BF16)
    er = jnp.exp(l_tot - lw_incl)
    kb = (k * er).astype(BF16)
    bbar = (bb * er).astype(BF16)
    vb = v.astype(BF16)
    ar = jnp.concatenate([at, rt.astype(BF16)], axis=0)
    mk = _dot_nt(ar, kh)
    mb = _dot_nt(ar, bh)
    zero = jnp.zeros((), F32)
    m_ak = jnp.where(strict, mk[:c], zero).astype(BF16)
    m_ab = jnp.where(strict, mb[:c], zero)
    m_rk = jnp.where(incl, mk[c:], zero).astype(BF16)
    m_rb = jnp.where(incl, mb[c:], zero).astype(BF16)
    p = m_ab
    t = jnp.where(eye, 1.0, zero) + m_ab
    steps = max(1, (c - 1).bit_length()) - 1
    for _ in range(steps):
        pb = p.astype(BF16)
        p = _dot(pb, pb)
        t = t + _dot(t.astype(BF16), p.astype(BF16))
    tb_ = t.astype(BF16)
    at2 = _dot(tb_, at)
    vp = _dot(tb_, _dot(m_ak, vb).astype(BF16))
    at2b = at2.astype(BF16)
    vpb = vp.astype(BF16)
    rp = rt + _dot(m_rb, at2b)
    y0 = _dot(m_rk, vb) + _dot(m_rb, vpb)
    g = _dot_tn(bbar, at2b)
    hp = _dot_tn(vpb, bbar) + _dot_tn(vb, kb)
    s0b = s0.astype(BF16)
    y = _dot_nt(rp.astype(BF16), s0b) + y0
    s_new = s0 * jnp.exp(l_tot) + _dot_nt(s0b, g.astype(BF16)) + hp
    return y, s_new


def _rw_scan_prompt_body(r_ref, lw_ref, k_ref, v_ref, aa_ref, bb_ref, y_ref, sout_ref, s_ref, *, tb):
    g = pl.program_id(2)
    c = RW_CHUNK
    n = RW_HEAD_DIM
    hp = LANES // n

    @pl.when(g == 0)
    def _():
        s_ref[...] = jnp.zeros_like(s_ref)

    row = _iota((c, c), 0)
    col = _iota((c, c), 1)
    masks = (col < row, col <= row, col == row)
    tri_incl = (col <= row).astype(BF16)

    def chunk(ci, carry):
        r0 = pl.multiple_of(ci * c, c)
        rows = pl.ds(r0, c)
        lw = lw_ref[rows, :]
        lw_incl = _dot_exact_rhs(tri_incl, lw)
        lw_excl = lw_incl - lw
        l_tot = lw_incl[c - 1:c, :]
        r, k, v, aa, bb = r_ref[rows, :], k_ref[rows, :], v_ref[rows, :], aa_ref[rows, :], bb_ref[rows, :]
        ys = []
        for hh in range(hp):
            sl = slice(hh * n, (hh + 1) * n)
            y, s_new = _rw_chunk_head(r[:, sl], lw_incl[:, sl], lw_excl[:, sl], l_tot[:, sl], k[:, sl],
                                      v[:, sl], aa[:, sl], bb[:, sl], s_ref[hh], masks)
            s_ref[hh] = s_new
            ys.append(y)
        y_ref[rows, :] = jnp.concatenate(ys, axis=1)
        return carry

    lax.fori_loop(0, tb // c, chunk, 0)
    sout_ref[0] = s_ref[...]


def _rw_scan_prompt(pre, *, batch, seq, tb):
    n_rows = batch * seq
    ng = seq // tb
    n = RW_HEAD_DIM
    hp = LANES // n
    npair = RW_WIDTH // LANES
    spec = pl.BlockSpec((tb, LANES), lambda b, p, g: (b * ng + g, p))
    return pl.pallas_call(
        functools.partial(_rw_scan_prompt_body, tb=tb),
        grid=(batch, npair, ng),
        in_specs=[spec] * 6,
        out_specs=[spec, pl.BlockSpec((1, hp, n, n), lambda b, p, g: (b, p, 0, 0))],
        out_shape=[jax.ShapeDtypeStruct((n_rows, RW_WIDTH), F32),
                   jax.ShapeDtypeStruct((batch, RW_HEADS, n, n), F32)],
        scratch_shapes=[pltpu.VMEM((hp, n, n), F32)],
        compiler_params=_cparams(("parallel", "parallel", "arbitrary")),
        name="rw_scan_prompt",
    )(*pre)


def _rw_scan_sample_body(s0_ref, r_ref, lw_ref, k_ref, v_ref, aa_ref, bb_ref,
                         y_ref, sout_ref, st_ref, xt_ref, yt_ref, *, nb, t_new):
    n = RW_HEAD_DIM
    hv = LANES
    st_ref[...] = s0_ref[...].T.reshape(hv, n, nb)
    srcs = (r_ref, lw_ref, k_ref, aa_ref, bb_ref, v_ref)
    for t in range(t_new):
        for vi, src in enumerate(srcs):
            x = src[pl.ds(t, nb, stride=t_new), :]
            if vi == 1:
                x = jnp.exp(x)
            xt_ref[vi, t] = x.T

    def row(i, carry):
        k0 = pl.multiple_of((i // n) * n, n)
        s = st_ref[i]
        for t in range(t_new):
            rr = xt_ref[0, t, pl.ds(k0, n), :]
            ww = xt_ref[1, t, pl.ds(k0, n), :]
            kk = xt_ref[2, t, pl.ds(k0, n), :]
            aa = xt_ref[3, t, pl.ds(k0, n), :]
            bb = xt_ref[4, t, pl.ds(k0, n), :]
            vv = xt_ref[5, t, pl.ds(i, 1), :]
            sa = jnp.sum(s * aa, axis=0, keepdims=True)
            s = s * ww + sa * bb + vv * kk
            yt_ref[t, pl.ds(i, 1), :] = jnp.sum(s * rr, axis=0, keepdims=True)
        st_ref[i] = s
        return carry

    lax.fori_loop(0, hv, row, 0)
    for t in range(t_new):
        y_ref[pl.ds(t, nb, stride=t_new), :] = yt_ref[t].T
    sout_ref[...] = st_ref[...].reshape(hv * n, nb).T


def _rw_scan_sample(pre, s0, *, nb, t_new):
    n = RW_HEAD_DIM
    tb = nb * t_new
    npair = RW_WIDTH // LANES
    sw = LANES * n
    s0f = s0.reshape(nb, RW_HEADS * n * n)
    spec = pl.BlockSpec((tb, LANES), lambda p: (0, p))
    sspec = pl.BlockSpec((nb, sw), lambda p: (0, p))
    y_new, s_new = pl.pallas_call(
        functools.partial(_rw_scan_sample_body, nb=nb, t_new=t_new),
        grid=(npair,),
        in_specs=[sspec] + [spec] * 6,
        out_specs=[spec, sspec],
        out_shape=[jax.ShapeDtypeStruct((tb, RW_WIDTH), F32), jax.ShapeDtypeStruct(s0f.shape, F32)],
        scratch_shapes=[pltpu.VMEM((LANES, n, nb), F32), pltpu.VMEM((6, t_new, LANES, nb), F32),
                        pltpu.VMEM((t_new, LANES, nb), F32)],
        compiler_params=_cparams(("arbitrary",)),
        name="rw_scan_sample",
    )(s0f, *pre)
    return y_new, s_new.reshape(s0.shape)


def _rw_post_body(y_ref, bonus_ref, g0, g1, g2, g3, gw_ref, gb_ref, seg_ref, o_ref):
    y = y_ref[...]
    seg = seg_ref[...]
    inv = 1.0 / RW_HEAD_DIM
    mean = _seg_sum(y, seg) * inv
    d = y - mean
    var = _seg_sum(d * d, seg) * inv
    yn = d * lax.rsqrt(var + RW_GN_EPS) * gw_ref[...] + gb_ref[...] + bonus_ref[...]
    gate = jnp.concatenate([g0[...], g1[...], g2[...], g3[...]], axis=1)
    o_ref[...] = (yn * _silu(gate)).astype(o_ref.dtype)


def _rw_post(y, bonus, proj, rw, *, tb):
    n_rows = y.shape[0]
    w = RW_WIDTH
    row = pl.BlockSpec((tb, w), lambda i: (i, 0))
    gspecs = [pl.BlockSpec((tb, LANES), functools.partial(lambda i, q: (i, OFF_RW_G // LANES + q), q=q))
              for q in range(w // LANES)]
    const = lambda shape: pl.BlockSpec(shape, lambda i: (0, 0))
    return pl.pallas_call(
        _rw_post_body,
        grid=(n_rows // tb,),
        in_specs=[row, row] + gspecs + [const((1, w)), const((1, w)), const((w, w))],
        out_specs=row,
        out_shape=jax.ShapeDtypeStruct((n_rows, w), BF16),
        compiler_params=_cparams(("parallel",)),
        name="rw_post",
    )(y, bonus, proj, proj, proj, proj, rw["gn_w"], rw["gn_b"], rw["seg"])


def _block_diag(w):
    nb, d, _ = w.shape
    eye = jnp.eye(nb, dtype=w.dtype)
    return (eye[:, None, :, None] * w[:, :, None, :]).reshape(nb * d, nb * d)


def _layer_tail(x, proj, osb, olru, pre, y_rw, p, lp, fg, final, nseq):
    orw = _rw_post(y_rw, pre[6], proj, lp["rw"], tb=lp["tm_in"])
    h = _outproj(x, osb, olru, orw, p, lp["w_out"], lp["ple_gate"], lp["ple_proj"], fg,
                 tm=lp["tm_out"], final=final)
    t = proj.shape[0] // nseq
    p3 = proj.reshape(nseq, t, D_IN_PROJ)
    heads = lambda off: p3[..., off:off + SB_WIDTH].reshape(nseq, t, SB_HEADS, SB_HEAD_DIM)
    hist = CONV_WIDTH - 1
    return h, (heads(OFF_SB_K), heads(OFF_SB_V), p3[:, t - hist:, OFF_LRU_X:OFF_LRU_G],
               p3[:, t - 1, OFF_RW:OFF_RW_G])


def _layer_prompt(x, p, lp, fg, *, batch, seq, final):
    proj = _inproj(x, lp["norm_g"], lp["w_in"], tm=lp["tm_in"], tn=lp["tn_in"])
    osb = _attn_prompt(proj, batch=batch, seq=seq, tq=lp["tq"])
    olru, h_last = _lru_prompt(proj, lp["lru"], batch=batch, seq=seq, tb=lp["lru_tb"])
    pre = _rw_pre_prompt(proj, lp["rw"], batch=batch, seq=seq, tb=lp["rw_tb"])
    y_rw, s_new = _rw_scan_prompt(pre[:6], batch=batch, seq=seq, tb=lp["rw_tb"])
    h, (k_new, v_new, conv_new, shift_new) = _layer_tail(x, proj, osb, olru, pre, y_rw, p, lp, fg, final, batch)
    return h, (k_new, v_new, h_last.reshape(batch, LRU_WIDTH), conv_new, s_new, shift_new)


def _layer_sample(x, p, cache_k, cache_v, page_table, lru_conv, lru_h, rw_shift, rw_s, lp, fg, *,
                  dec_batch, t_new, final):
    proj = _inproj(x, lp["norm_g"], lp["w_in"], tm=lp["tm_in"], tn=lp["tn_in"])
    osb = _attn_sample(proj, cache_k, cache_v, page_table, dec_batch=dec_batch, t_new=t_new,
                       bb=lp["attn_bb"])
    olru, h_last = _lru_sample(proj, lru_conv, lru_h, lp["lru"], nb=dec_batch, t_new=t_new)
    pre = _rw_pre_sample(proj, rw_shift, lp["rw"], nb=dec_batch, t_new=t_new)
    y_rw, s_new = _rw_scan_sample(pre[:6], rw_s, nb=dec_batch, t_new=t_new)
    h, (k_new, v_new, conv_new, shift_new) = _layer_tail(x, proj, osb, olru, pre, y_rw, p, lp, fg, final,
                                                          dec_batch)
    return h, (k_new, v_new, h_last, conv_new, s_new, shift_new)


def _layer_params(l, norm_g, w_in, w_out, lru_conv_w, lru_conv_b, lru_gate_a_w, lru_gate_a_b,
                  lru_gate_x_w, lru_gate_x_b, lru_lambda, rw_mu, rw_w0, rw_w2, rw_a0, rw_a2,
                  rw_k_k, rw_k_a, rw_r_k, rw_gn_w, rw_gn_b, ple_proj, ple_gate):
    row = lambda a: a[l].reshape(1, -1)
    w = RW_WIDTH
    zeros = jnp.zeros((RW_LORA, w), F32)
    head_id = jnp.arange(w) // RW_HEAD_DIM
    lru = dict(conv_w=lru_conv_w[l], conv_b=row(lru_conv_b),
               gate_w=jnp.concatenate([_block_diag(lru_gate_a_w[l]), _block_diag(lru_gate_x_w[l])],
                                      axis=1).astype(BF16),
               gate_a_b=row(lru_gate_a_b), gate_x_b=row(lru_gate_x_b), lam=row(lru_lambda))
    rw = dict(mu=row(rw_mu), w0=row(rw_w0),
              w2=jnp.concatenate([rw_w2[l], zeros], axis=0).astype(BF16),
              a0=row(rw_a0),
              a2=jnp.concatenate([zeros, rw_a2[l]], axis=0).astype(BF16),
              k_k=row(rw_k_k), k_a=row(rw_k_a), r_k=row(rw_r_k),
              gn_w=row(rw_gn_w), gn_b=row(rw_gn_b),
              seg=(head_id[:, None] == head_id[None, :]).astype(BF16))
    return dict(norm_g=row(norm_g), w_in=w_in[l].astype(BF16), w_out=w_out[l].astype(BF16),
                ple_proj=ple_proj[l].astype(BF16), ple_gate=ple_gate[l].astype(BF16), lru=lru, rw=rw)


def _tiles(seq):
    return dict(tm_in=512, tn_in=D_IN_PROJ // 3, tq=128, attn_bb=8, lru_tb=min(512, seq),
                rw_tb=min(512, seq), tm_out=256)


def kernel(x_prompt, x_sample, p_prompt, p_sample, cache_sb_k, cache_sb_v, page_table, state_lru_h, state_lru_conv, state_rw_S, state_rw_shift, norm_g, w_in, w_out, lru_conv_w, lru_conv_b, lru_gate_a_w, lru_gate_a_b, lru_gate_x_w, lru_gate_x_b, lru_lambda, rw_mu, rw_w0, rw_w2, rw_a0, rw_a2, rw_k_k, rw_k_a, rw_r_k, rw_gn_w, rw_gn_b, ple_proj, ple_gate, final_norm_g):
    batch, seq, d = x_prompt.shape
    dec_batch, t_new, _ = x_sample.shape
    depth = w_in.shape[0]
    n_p, n_s = batch * seq, dec_batch * t_new
    xp = x_prompt.reshape(n_p, d)
    xs = x_sample.reshape(n_s, d)
    fg = final_norm_g.reshape(1, d)
    outs_p, outs_s = [], []
    for l in range(depth):
        lp = _layer_params(l, norm_g, w_in, w_out, lru_conv_w, lru_conv_b, lru_gate_a_w, lru_gate_a_b,
                           lru_gate_x_w, lru_gate_x_b, lru_lambda, rw_mu, rw_w0, rw_w2, rw_a0, rw_a2,
                           rw_k_k, rw_k_a, rw_r_k, rw_gn_w, rw_gn_b, ple_proj, ple_gate)
        lp.update(_tiles(seq))
        final = l == depth - 1
        xp, st_p = _layer_prompt(xp, p_prompt[l].reshape(n_p, -1), lp, fg, batch=batch, seq=seq, final=final)
        xs, st_s = _layer_sample(xs, p_sample[l].reshape(n_s, -1), cache_sb_k[l], cache_sb_v[l], page_table,
                                 state_lru_conv[l], state_lru_h[l], state_rw_shift[l], state_rw_S[l], lp, fg,
                                 dec_batch=dec_batch, t_new=t_new, final=final)
        outs_p.append(st_p)
        outs_s.append(st_s)
    y_prompt = xp.reshape(batch, seq, d)
    y_sample = xs.reshape(dec_batch, t_new, d)
    stack = lambda outs, k: jnp.stack([o[k] for o in outs])
    return (y_prompt, y_sample, stack(outs_p, 0), stack(outs_p, 1), stack(outs_s, 0), stack(outs_s, 1),
            stack(outs_p, 2), stack(outs_s, 2), stack(outs_p, 3), stack(outs_s, 3),
            stack(outs_p, 4), stack(outs_s, 4), stack(outs_p, 5), stack(outs_s, 5))
```

```python
import functools
import math

import jax
import jax.numpy as jnp
from jax import lax
from jax.experimental import pallas as pl
from jax.experimental.pallas import tpu as pltpu

F32 = jnp.float32
BF16 = jnp.bfloat16

D_MODEL = 2048
D_PLE = 256
RMS_EPS = 1e-6
SB_HEADS = 8
SB_HEAD_DIM = 128
SB_WIDTH = SB_HEADS * SB_HEAD_DIM
LRU_WIDTH = 512
LRU_BLOCKS = 8
CONV_WIDTH = 4
LRU_C = 8.0
RW_WIDTH = 512
RW_HEAD_DIM = 64
RW_HEADS = 8
RW_LORA = 64
RW_GN_EPS = 64e-5
RW_SHIFT_WIDTH = 3 * RW_WIDTH + 2 * RW_LORA
OFF_SB_Q = 0
OFF_SB_K = OFF_SB_Q + SB_WIDTH
OFF_SB_V = OFF_SB_K + SB_WIDTH
OFF_SB_G = OFF_SB_V + SB_WIDTH
OFF_LRU_X = OFF_SB_G + SB_WIDTH
OFF_LRU_G = OFF_LRU_X + LRU_WIDTH
OFF_RW = OFF_LRU_G + LRU_WIDTH
OFF_RW_XWA = OFF_RW + 3 * RW_WIDTH
OFF_RW_G = OFF_RW + RW_SHIFT_WIDTH
D_IN_PROJ = OFF_RW_G + RW_WIDTH

LANES = 128
SUBLANES = 8
VMEM_LIMIT = 56 * 1024 * 1024

LOG_ZERO = -110.0


def _cparams(sem, vmem=VMEM_LIMIT):
    return pltpu.CompilerParams(dimension_semantics=sem, vmem_limit_bytes=vmem)


def _dot(a, b):
    return jnp.dot(a, b, preferred_element_type=F32)


def _dot_nt(a, b):
    return lax.dot_general(a, b, (((1,), (1,)), ((), ())), preferred_element_type=F32)


def _dot_tn(a, b):
    return lax.dot_general(a, b, (((0,), (0,)), ((), ())), preferred_element_type=F32)


def _split3(x):
    x1 = x.astype(BF16)
    r1 = x - x1.astype(F32)
    x2 = r1.astype(BF16)
    x3 = (r1 - x2.astype(F32)).astype(BF16)
    return x1, x2, x3


def _dot_exact_lhs(x, m):
    x1, x2, x3 = _split3(x)
    return _dot(x1, m) + _dot(x2, m) + _dot(x3, m)


def _dot_exact_rhs(m, x):
    x1, x2, x3 = _split3(x)
    return _dot(m, x1) + _dot(m, x2) + _dot(m, x3)


def _each(f, *lists):
    return [f(*xs) for xs in zip(*lists)]


def _iota(shape, dim):
    return lax.broadcasted_iota(jnp.int32, shape, dim)


def _silu(x):
    return x * jax.nn.sigmoid(x)


def _softplus(x):
    return jnp.maximum(x, 0.0) + jnp.log1p(jnp.exp(-jnp.abs(x)))


EXPM1_SERIES_TERMS = 9
EXPM1_SERIES_RANGE = 0.25


def _neg_expm1(x, ex):
    p = 1.0 + x * (1.0 / EXPM1_SERIES_TERMS)
    for n in range(EXPM1_SERIES_TERMS - 1, 1, -1):
        p = 1.0 + (x * (1.0 / n)) * p
    return jnp.where(x > -EXPM1_SERIES_RANGE, -x * p, 1.0 - ex)


def _log_sigmoid_pair(z):
    t = jnp.log1p(jnp.exp(-jnp.abs(z)))
    return jnp.minimum(z, 0.0) - t, -jnp.maximum(z, 0.0) - t


def _inproj_body(x_ref, g_ref, w_ref, o_ref, u_ref):
    @pl.when(pl.program_id(1) == 0)
    def _():
        x = x_ref[...]
        ms = jnp.mean(x * x, axis=-1, keepdims=True)
        u_ref[...] = (x * lax.rsqrt(ms + RMS_EPS) * g_ref[...]).astype(BF16)

    o_ref[...] = _dot(u_ref[...], w_ref[...])


def _inproj(x, g, w, *, tm, tn):
    n, d = x.shape
    dout = w.shape[1]
    return pl.pallas_call(
        _inproj_body,
        grid=(n // tm, dout // tn),
        in_specs=[pl.BlockSpec((tm, d), lambda i, j: (i, 0)),
                  pl.BlockSpec((1, d), lambda i, j: (0, 0)),
                  pl.BlockSpec((d, tn), lambda i, j: (0, j))],
        out_specs=pl.BlockSpec((tm, tn), lambda i, j: (i, j)),
        out_shape=jax.ShapeDtypeStruct((n, dout), F32),
        scratch_shapes=[pltpu.VMEM((tm, d), BF16)],
        compiler_params=_cparams(("parallel", "arbitrary")),
        name="inproj",
    )(x, g, w)


def _outproj_body(x_ref, osb_ref, olru_ref, orw_ref, p_ref, w1_ref, w2_ref, w3_ref,
                  wg_ref, wp_ref, fg_ref, o_ref, *, final):
    h = x_ref[...] + _dot(osb_ref[...], w1_ref[...]) + _dot(olru_ref[...], w2_ref[...]) \
        + _dot(orw_ref[...], w3_ref[...])
    gate = _dot(h.astype(BF16), wg_ref[...])
    pp = _dot(p_ref[...].astype(BF16), wp_ref[...])
    h = h + jax.nn.sigmoid(gate) * pp
    if final:
        ms = jnp.mean(h * h, axis=-1, keepdims=True)
        h = h * lax.rsqrt(ms + RMS_EPS) * fg_ref[...]
    o_ref[...] = h


def _outproj(x, osb, olru, orw, p, w_out, w_gate, w_ple, fg, *, tm, final):
    n, d = x.shape
    wsb, wlru, wrw = osb.shape[1], olru.shape[1], orw.shape[1]
    dp = p.shape[1]
    row = lambda w: pl.BlockSpec((tm, w), lambda i: (i, 0))
    const = lambda shape, idx: pl.BlockSpec(shape, lambda i: idx)
    return pl.pallas_call(
        functools.partial(_outproj_body, final=final),
        grid=(n // tm,),
        in_specs=[row(d), row(wsb), row(wlru), row(wrw), row(dp),
                  const((wsb, d), (0, 0)),
                  const((wlru, d), (wsb // wlru, 0)),
                  const((wrw, d), ((wsb + wlru) // wrw, 0)),
                  const((d, d), (0, 0)),
                  const((dp, d), (0, 0)),
                  const((1, d), (0, 0))],
        out_specs=row(d),
        out_shape=jax.ShapeDtypeStruct((n, d), F32),
        compiler_params=_cparams(("parallel",)),
        name="outproj",
    )(x, osb, olru, orw, p, w_out, w_out, w_out, w_gate, w_ple, fg)


ATTN_NEAR_BLOCKS = 3


def _sb_block(qh, kf, vf, r_old, mask, tri, scale):
    hd = SB_HEAD_DIM
    heads = range(len(qh))
    kh = [kf[:, h * hd:(h + 1) * hd].astype(BF16) for h in heads]
    vh = [vf[:, h * hd:(h + 1) * hd].astype(BF16) for h in heads]
    z = _each(lambda a, b: _dot_nt(a, b) * scale, qh, kh)
    ls = _each(_log_sigmoid_pair, z)
    lsz = [x[0] for x in ls]
    lstay = [x[1] for x in ls]
    if mask is not None:
        lstay = _each(lambda x: jnp.where(mask, x, 0.0), lstay)
    s_excl = _each(lambda x: _dot_exact_lhs(x, tri), lstay)
    w = _each(lambda a, s, r: jnp.exp(a + s + r), lsz, s_excl, r_old)
    if mask is not None:
        w = _each(lambda x: jnp.where(mask, x, 0.0), w)
    pv = _each(lambda x, v: _dot(x.astype(BF16), v), w, vh)
    r_new = _each(lambda r, s, x: r + s[:, 0:1] + x[:, 0:1], r_old, s_excl, lstay)
    return pv, r_new


def _all_done(r_list):
    r_max = functools.reduce(jnp.maximum, r_list)
    return (jnp.max(r_max) < LOG_ZERO).astype(jnp.int32)


def _attn_prompt_body(*refs, tq, nq, scale):
    near = ATTN_NEAR_BLOCKS
    q_ref, g_ref = refs[0], refs[1]
    k_refs, v_refs = refs[2:2 + near], refs[2 + near:2 + 2 * near]
    proj_ref, o_ref, kbuf, vbuf, sem, acc_ref, r_ref = refs[2 + 2 * near:]
    b = pl.program_id(0)
    i = pl.program_id(1)
    hd = SB_HEAD_DIM
    heads = range(SB_HEADS)
    q = q_ref[...].astype(BF16)
    qh = [q[:, h * hd:(h + 1) * hd] for h in heads]
    row = _iota((tq, tq), 0)
    col = _iota((tq, tq), 1)
    tri = (row > col).astype(BF16)
    r = [jnp.zeros((tq, 1), F32) for _ in heads]
    acc = [jnp.zeros((tq, hd), F32) for _ in heads]
    for d in range(near):
        mask = (col < row) if d == 0 else ((row * 0 + i) >= d)
        pv, r = _sb_block(qh, k_refs[d][...], v_refs[d][...], r, mask, tri, scale)
        acc = _each(lambda a, x: a + x, acc, pv)
    for h in heads:
        acc_ref[h] = acc[h]
        r_ref[h] = r[h]

    def copies(j):
        rows = pl.ds(pl.multiple_of((b * nq + j) * tq, tq), tq)
        return (pltpu.make_async_copy(proj_ref.at[rows, pl.ds(OFF_SB_K, SB_WIDTH)], kbuf, sem.at[0]),
                pltpu.make_async_copy(proj_ref.at[rows, pl.ds(OFF_SB_V, SB_WIDTH)], vbuf, sem.at[1]))

    def cond(c):
        j, done = c
        return jnp.logical_and(j >= 0, done == 0)

    def body(c):
        j, _ = c
        for cp in copies(j):
            cp.start()
        for cp in copies(j):
            cp.wait()
        pv, r_new = _sb_block(qh, kbuf[...], vbuf[...], [r_ref[h] for h in heads], None, tri, scale)
        for h in heads:
            acc_ref[h] += pv[h]
            r_ref[h] = r_new[h]
        return j - 1, _all_done(r_new)

    lax.while_loop(cond, body, (i - near, _all_done(r)))
    out = jnp.concatenate([acc_ref[h] for h in heads], axis=1)
    o_ref[...] = (out * _silu(g_ref[...])).astype(o_ref.dtype)


def _attn_prompt(proj, *, batch, seq, tq):
    n_rows = batch * seq
    nq = seq // tq
    w = SB_WIDTH
    scale = 1.0 / math.sqrt(SB_HEAD_DIM)
    near = ATTN_NEAR_BLOCKS
    cur = lambda off: pl.BlockSpec((tq, w), lambda b, i: (b * nq + i, off // w))
    back = lambda off, d: pl.BlockSpec((tq, w), lambda b, i: (b * nq + jnp.maximum(i - d, 0), off // w))
    return pl.pallas_call(
        functools.partial(_attn_prompt_body, tq=tq, nq=nq, scale=scale),
        grid=(batch, nq),
        in_specs=[cur(OFF_SB_Q), cur(OFF_SB_G)]
        + [back(OFF_SB_K, d) for d in range(near)] + [back(OFF_SB_V, d) for d in range(near)]
        + [pl.BlockSpec(memory_space=pl.ANY)],
        out_specs=pl.BlockSpec((tq, w), lambda b, i: (b * nq + i, 0)),
        out_shape=jax.ShapeDtypeStruct((n_rows, w), BF16),
        scratch_shapes=[pltpu.VMEM((tq, w), F32), pltpu.VMEM((tq, w), F32), pltpu.SemaphoreType.DMA((2,)),
                        pltpu.VMEM((SB_HEADS, tq, SB_HEAD_DIM), F32), pltpu.VMEM((SB_HEADS, tq, 1), F32)],
        compiler_params=_cparams(("parallel", "arbitrary")),
        name="attn_prompt",
    )(*([proj] * (2 + 2 * near + 1)))


ATTN_NEAR_PAGES = 2


def _sb_block_seqs(qh, kh, vh, r_old, mask, tri, scale):
    seqs = range(len(kh))
    heads = range(len(kh[0]))
    t = qh[0][0].shape[0]
    z = [[_dot_nt(qh[s][h], kh[s][h].astype(BF16)) for h in heads] for s in seqs]
    ls = [_log_sigmoid_pair(jnp.concatenate(z[s], axis=0) * scale) for s in seqs]
    lsz = [x[0] for x in ls]
    lstay = [x[1] for x in ls]
    if mask is not None:
        lstay = _each(lambda x: jnp.where(mask, x, 0.0), lstay)
    s_excl = _each(lambda x: _dot_exact_lhs(x, tri), lstay)
    w = _each(lambda a, sx, r: jnp.exp(a + sx + r), lsz, s_excl, r_old)
    if mask is not None:
        w = _each(lambda x: jnp.where(mask, x, 0.0), w)
    pv = [[_dot(w[s][h * t:(h + 1) * t, :].astype(BF16), vh[s][h].astype(BF16)) for h in heads]
          for s in seqs]
    r_new = _each(lambda r, sx, x: r + sx[:, 0:1] + x[:, 0:1], r_old, s_excl, lstay)
    return pv, r_new


def _attn_sample_body(pt_ref, q_ref, kn_ref, vn_ref, g_ref, kc_ref, vc_ref, o_ref,
                      kbuf, vbuf, sem, acc_ref, r_ref, of_ref, *, layer, bb, t_new, n_pages, page, scale):
    i = pl.program_id(0)
    nsteps = pl.num_programs(0)
    heads = range(SB_HEADS)
    seqs = range(bb)
    hq = SB_HEADS * t_new
    hd = SB_HEAD_DIM
    near = min(ATTN_NEAR_PAGES, n_pages)
    extra_slot = 2 * bb * near
    par = i % 2

    def page_copies(b, j, slot):
        pg = pt_ref[b, j]
        return (pltpu.make_async_copy(kc_ref.at[layer, pg], kbuf.at[slot], sem.at[0, slot]),
                pltpu.make_async_copy(vc_ref.at[layer, pg], vbuf.at[slot], sem.at[1, slot]))

    def near_slot(parity, s, pi):
        return (parity * bb + s) * near + pi

    def near_copies(step, parity):
        return [c for s in seqs for pi in range(near)
                for c in page_copies(step * bb + s, n_pages - near + pi, near_slot(parity, s, pi))]

    @pl.when(i == 0)
    def _():
        for c in near_copies(i, par):
            c.start()

    @pl.when(i + 1 < nsteps)
    def _():
        for c in near_copies(i + 1, 1 - par):
            c.start()

    tri_new = (_iota((t_new, t_new), 0) > _iota((t_new, t_new), 1)).astype(BF16)
    npg = near * page
    tri_near = (_iota((npg, npg), 0) > _iota((npg, npg), 1)).astype(BF16)
    tri_page = tri_near[:page, :page]
    new_mask = _iota((hq, t_new), 1) < (_iota((hq, t_new), 0) % t_new)
    split = lambda x: [x[:, h * hd:(h + 1) * hd] for h in heads]
    head_rows = lambda buf, slot, h: buf[slot, pl.ds(h, page, stride=SB_HEADS), :]

    qh = [split(q_ref[s * t_new:(s + 1) * t_new, :].astype(BF16)) for s in seqs]
    r = [jnp.zeros((hq, 1), F32) for _ in seqs]
    kh = [split(kn_ref[s * t_new:(s + 1) * t_new, :]) for s in seqs]
    vh = [split(vn_ref[s * t_new:(s + 1) * t_new, :]) for s in seqs]
    acc, r = _sb_block_seqs(qh, kh, vh, r, new_mask, tri_new, scale)
    for c in near_copies(i, par):
        c.wait()
    page_rows = lambda buf, s, h: jnp.concatenate(
        [head_rows(buf, near_slot(par, s, pi), h) for pi in range(near)], axis=0)
    kh = [[page_rows(kbuf, s, h) for h in heads] for s in seqs]
    vh = [[page_rows(vbuf, s, h) for h in heads] for s in seqs]
    pv, r = _sb_block_seqs(qh, kh, vh, r, None, tri_near, scale)
    for s in seqs:
        for h in heads:
            acc_ref[s, h] = acc[s][h] + pv[s][h]
        r_ref[s] = r[s]

    for s in seqs:
        b = i * bb + s

        def cond(c):
            j, done = c
            return jnp.logical_and(j >= 0, done == 0)

        def body(c, b=b, s=s):
            j, _ = c
            for cp in page_copies(b, j, extra_slot):
                cp.start()
            for cp in page_copies(b, j, extra_slot):
                cp.wait()
            kj = [[head_rows(kbuf, extra_slot, h) for h in heads]]
            vj = [[head_rows(vbuf, extra_slot, h) for h in heads]]
            pv_j, r_j = _sb_block_seqs([qh[s]], kj, vj, [r_ref[s]], None, tri_page, scale)
            for h in heads:
                acc_ref[s, h] += pv_j[0][h]
            r_ref[s] = r_j[0]
            return j - 1, _all_done(r_j)

        lax.while_loop(cond, body, (jnp.int32(n_pages - near - 1), _all_done([r[s]])))
        o_b = jnp.concatenate([acc_ref[s, h] for h in heads], axis=1)
        rows = slice(s * t_new, (s + 1) * t_new)
        of_ref[rows, :] = o_b * _silu(g_ref[rows, :])
    o_ref[...] = of_ref[...].astype(o_ref.dtype)


def _attn_sample(proj, cache_k, cache_v, page_table, *, layer, dec_batch, t_new, bb):
    depth, n_phys, page = cache_k.shape[:3]
    n_pages = page_table.shape[1]
    tb = bb * t_new
    spec = lambda off: pl.BlockSpec((tb, SB_WIDTH), lambda i, pt: (i, off // SB_WIDTH))
    hq = SB_HEADS * t_new
    nslots = 2 * bb * min(ATTN_NEAR_PAGES, n_pages) + 1
    cache_k = cache_k.reshape(depth, n_phys, page * SB_HEADS, SB_HEAD_DIM)
    cache_v = cache_v.reshape(depth, n_phys, page * SB_HEADS, SB_HEAD_DIM)
    pagebuf = pltpu.VMEM((nslots, page * SB_HEADS, SB_HEAD_DIM), F32)
    grid_spec = pltpu.PrefetchScalarGridSpec(
        num_scalar_prefetch=1,
        grid=(dec_batch // bb,),
        in_specs=[spec(OFF_SB_Q), spec(OFF_SB_K), spec(OFF_SB_V), spec(OFF_SB_G),
                  pl.BlockSpec(memory_space=pl.ANY), pl.BlockSpec(memory_space=pl.ANY)],
        out_specs=pl.BlockSpec((tb, SB_WIDTH), lambda i, pt: (i, 0)),
        scratch_shapes=[pagebuf, pagebuf, pltpu.SemaphoreType.DMA((2, nslots)),
                        pltpu.VMEM((bb, SB_HEADS, t_new, SB_HEAD_DIM), F32), pltpu.VMEM((bb, hq, 1), F32),
                        pltpu.VMEM((tb, SB_WIDTH), F32)],
    )
    return pl.pallas_call(
        functools.partial(_attn_sample_body, layer=layer, bb=bb, t_new=t_new, n_pages=n_pages, page=page,
                          scale=1.0 / math.sqrt(SB_HEAD_DIM)),
        grid_spec=grid_spec,
        out_shape=jax.ShapeDtypeStruct((dec_batch * t_new, SB_WIDTH), BF16),
        compiler_params=_cparams(("arbitrary",)),
        name="attn_sample",
    )(page_table, proj, proj, proj, proj, cache_k, cache_v)


def _lru_gates(xc, gw_ref, ga_b_ref, gx_b_ref, lam_ref):
    w = xc.shape[-1]
    gates = _dot(xc.astype(BF16), gw_ref[...])
    r = jax.nn.sigmoid(gates[:, :w] + ga_b_ref[...])
    i = jax.nn.sigmoid(gates[:, w:] + gx_b_ref[...])
    log_a = -LRU_C * r * _softplus(-lam_ref[...])
    a = jnp.exp(log_a)
    mult = jnp.sqrt(_neg_expm1(2.0 * log_a, a * a))
    return a, mult * (i * xc)


def _lru_prompt_body(x_ref, g_ref, cw_ref, cb_ref, gw_ref, ga_b_ref, gx_b_ref, lam_ref,
                     o_ref, hl_ref, xp_ref, a_ref, b_ref, h_ref, *, tb):
    c = pl.program_id(1)
    pad = SUBLANES
    hist = CONV_WIDTH - 1

    @pl.when(c == 0)
    def _():
        xp_ref[0:pad, :] = jnp.zeros((pad, LRU_WIDTH), F32)
        h_ref[...] = jnp.zeros_like(h_ref)

    xb = x_ref[...]
    xp_ref[pad:pad + tb, :] = xb
    xc = cb_ref[...] + xp_ref[pad - hist:pad - hist + tb, :] * cw_ref[0:1, :]
    for j in range(1, CONV_WIDTH):
        xc = xc + xp_ref[pad - hist + j:pad - hist + j + tb, :] * cw_ref[j:j + 1, :]
    xp_ref[pad - hist:pad, :] = xb[tb - hist:tb, :]

    a, bt = _lru_gates(xc, gw_ref, ga_b_ref, gx_b_ref, lam_ref)
    rowg = _iota((tb, LRU_WIDTH), 0) % SUBLANES
    s = 1
    while s < SUBLANES:
        a_sh = pltpu.roll(a, s, 0)
        b_sh = pltpu.roll(bt, s, 0)
        ok = rowg >= s
        bt = jnp.where(ok, a * b_sh + bt, bt)
        a = jnp.where(ok, a * a_sh, a)
        s *= 2
    a_ref[...] = a
    b_ref[...] = bt

    def group(gi, h):
        r0 = pl.multiple_of(gi * SUBLANES, SUBLANES)
        hs = a_ref[pl.ds(r0, SUBLANES), :] * h + b_ref[pl.ds(r0, SUBLANES), :]
        b_ref[pl.ds(r0, SUBLANES), :] = hs
        return jnp.broadcast_to(hs[SUBLANES - 1:SUBLANES, :], (SUBLANES, LRU_WIDTH))

    h = lax.fori_loop(0, tb // SUBLANES, group, h_ref[...])
    h_ref[...] = h
    hl_ref[0] = h[0:1, :]
    o_ref[...] = (b_ref[...] * _silu(g_ref[...])).astype(o_ref.dtype)


def _lru_prompt(proj, lw, *, batch, seq, tb):
    n_rows = batch * seq
    nc = seq // tb
    w = LRU_WIDTH
    const = lambda shape: pl.BlockSpec(shape, lambda b, c: (0,) * len(shape))
    return pl.pallas_call(
        functools.partial(_lru_prompt_body, tb=tb),
        grid=(batch, nc),
        in_specs=[pl.BlockSpec((tb, w), lambda b, c: (b * nc + c, OFF_LRU_X // w)),
                  pl.BlockSpec((tb, w), lambda b, c: (b * nc + c, OFF_LRU_G // w)),
                  const((CONV_WIDTH, w)), const((1, w)), const((w, 2 * w)),
                  const((1, w)), const((1, w)), const((1, w))],
        out_specs=[pl.BlockSpec((tb, w), lambda b, c: (b * nc + c, 0)),
                   pl.BlockSpec((1, 1, w), lambda b, c: (b, 0, 0))],
        out_shape=[jax.ShapeDtypeStruct((n_rows, w), BF16),
                   jax.ShapeDtypeStruct((batch, 1, w), F32)],
        scratch_shapes=[pltpu.VMEM((SUBLANES + tb, w), F32), pltpu.VMEM((tb, w), F32),
                        pltpu.VMEM((tb, w), F32), pltpu.VMEM((SUBLANES, w), F32)],
        compiler_params=_cparams(("arbitrary", "arbitrary")),
        name="lru_prompt",
    )(proj, proj, lw["conv_w"], lw["conv_b"], lw["gate_w"], lw["gate_a_b"], lw["gate_x_b"], lw["lam"])


def _lru_sample_body(*refs, nb, t_new):
    ncb = LRU_WIDTH // LANES
    x_refs, cbuf_refs = refs[:ncb], refs[ncb:2 * ncb]
    (g_ref, h0_ref, cw_ref, cb_ref, gw_ref, ga_b_ref, gx_b_ref, lam_ref,
     o_ref, hl_ref, of_ref) = refs[2 * ncb:]
    hist = CONV_WIDTH - 1
    gather = lambda rs, start, stride: jnp.concatenate(
        [r[pl.ds(start, nb, stride=stride), :] for r in rs], axis=1)
    xs = [gather(cbuf_refs, j, hist) for j in range(hist)]
    xs += [gather(x_refs, t, t_new) for t in range(t_new)]
    h = h0_ref[...]
    for t in range(t_new):
        xc = cb_ref[...]
        for j in range(CONV_WIDTH):
            xc = xc + xs[t + j] * cw_ref[j:j + 1, :]
        a, bt = _lru_gates(xc, gw_ref, ga_b_ref, gx_b_ref, lam_ref)
        h = a * h + bt
        for cb in range(of_ref.shape[0]):
            of_ref[cb, pl.ds(t, nb, stride=t_new), :] = h[:, cb * LANES:(cb + 1) * LANES]
    hl_ref[...] = h
    hs = jnp.concatenate([of_ref[cb] for cb in range(of_ref.shape[0])], axis=1)
    o_ref[...] = (hs * _silu(g_ref[...])).astype(o_ref.dtype)


def _lru_sample(proj, conv_buf, h0, lw, *, nb, t_new):
    w = LRU_WIDTH
    tb = nb * t_new
    const = lambda shape: pl.BlockSpec(shape, lambda i: (0,) * len(shape))
    ncb = w // LANES
    hist = CONV_WIDTH - 1
    colblk = lambda rows, c: pl.BlockSpec((rows, LANES), lambda i: (0, c))
    return pl.pallas_call(
        functools.partial(_lru_sample_body, nb=nb, t_new=t_new),
        grid=(1,),
        in_specs=[colblk(tb, OFF_LRU_X // LANES + q) for q in range(ncb)]
        + [colblk(nb * hist, q) for q in range(ncb)]
        + [pl.BlockSpec((tb, w), lambda i: (0, OFF_LRU_G // w)), const((nb, w)),
           const((CONV_WIDTH, w)), const((1, w)), const((w, 2 * w)),
           const((1, w)), const((1, w)), const((1, w))],
        out_specs=[const((tb, w)), const((nb, w))],
        out_shape=[jax.ShapeDtypeStruct((tb, w), BF16),
                   jax.ShapeDtypeStruct((nb, w), F32)],
        scratch_shapes=[pltpu.VMEM((ncb, tb, LANES), F32)],
        compiler_params=_cparams(("arbitrary",)),
        name="lru_sample",
    )(*([proj] * ncb), *([conv_buf.reshape(nb * hist, w)] * ncb), proj, h0,
      lw["conv_w"], lw["conv_b"], lw["gate_w"], lw["gate_a_b"], lw["gate_x_b"], lw["lam"])


def _seg_sum(x, seg):
    return _dot_exact_lhs(x, seg)


def _rw_prepare(z_r, z_k, z_v, z_x, p_r, p_k, p_v, p_x, rp):
    mu = rp["mu"]
    w = RW_WIDTH
    mix = lambda z, p, lo, hi: z + mu[:, lo:hi] * (p - z)
    r = mix(z_r, p_r, 0, w)
    k = mix(z_k, p_k, w, 2 * w)
    v = mix(z_v, p_v, 2 * w, 3 * w)
    x = mix(z_x, p_x, 3 * w, 3 * w + 2 * RW_LORA)
    seg = rp["seg"]
    w_log = -_softplus(-(rp["w0"] + _dot(jnp.tanh(x).astype(BF16), rp["w2"]))) - 0.5
    logw = -jnp.exp(w_log)
    a = jax.nn.sigmoid(rp["a0"] + _dot(x.astype(BF16), rp["a2"]))
    kk = k * rp["k_k"]
    kk = kk * lax.rsqrt(_seg_sum(kk * kk, seg) + 1e-12)
    k2 = k * (1.0 + (a - 1.0) * rp["k_a"])
    bonus = _seg_sum(r * k2 * rp["r_k"], seg) * v
    return r, logw, k2, v, -kk, kk * a, bonus


def _rw_params(refs):
    names = ("mu", "w0", "w2", "a0", "a2", "k_k", "k_a", "r_k", "seg")
    return {n: r[...] for n, r in zip(names, refs)}


def _rw_param_specs(idx):
    w = RW_WIDTH
    shapes = [(1, RW_SHIFT_WIDTH), (1, w), (2 * RW_LORA, w), (1, w), (2 * RW_LORA, w),
              (1, w), (1, w), (1, w), (w, w)]
    return [pl.BlockSpec(s, idx) for s in shapes]


def _rw_param_args(rw):
    return [rw[n] for n in ("mu", "w0", "w2", "a0", "a2", "k_k", "k_a", "r_k", "seg")]


def _rw_pre_prompt_body(zr_ref, zk_ref, zv_ref, zx_ref, *rest, tb):
    prm = _rw_params(rest[:9])
    outs = rest[9:16]
    last = rest[16:20]
    c = pl.program_id(1)

    @pl.when(c == 0)
    def _():
        for l in last:
            l[...] = jnp.zeros_like(l)

    zs = [zr_ref[...], zk_ref[...], zv_ref[...], zx_ref[...]]
    prevs = []
    for z, l in zip(zs, last):
        first = _iota(z.shape, 0) == 0
        prevs.append(jnp.where(first, jnp.broadcast_to(l[0:1, :], z.shape), pltpu.roll(z, 1, 0)))
        l[...] = jnp.broadcast_to(z[tb - 1:tb, :], l.shape)
    res = _rw_prepare(*zs, *prevs, prm)
    for o, val in zip(outs, res):
        o[...] = val


def _rw_pre_specs(row_idx):
    w = RW_WIDTH
    xw = 2 * RW_LORA
    return [(w, OFF_RW // w), (w, OFF_RW // w + 1), (w, OFF_RW // w + 2), (xw, OFF_RW_XWA // xw)]


def _rw_pre_prompt(proj, rw, *, batch, seq, tb):
    n_rows = batch * seq
    nc = seq // tb
    w = RW_WIDTH
    cols = _rw_pre_specs(None)
    in_specs = [pl.BlockSpec((tb, cw), functools.partial(lambda b, c, cb: (b * nc + c, cb), cb=cb))
                for cw, cb in cols]
    in_specs += _rw_param_specs(lambda b, c: (0, 0))
    out_spec = pl.BlockSpec((tb, w), lambda b, c: (b * nc + c, 0))
    return pl.pallas_call(
        functools.partial(_rw_pre_prompt_body, tb=tb),
        grid=(batch, nc),
        in_specs=in_specs,
        out_specs=[out_spec] * 7,
        out_shape=[jax.ShapeDtypeStruct((n_rows, w), F32)] * 7,
        scratch_shapes=[pltpu.VMEM((SUBLANES, cw), F32) for cw, _ in cols],
        compiler_params=_cparams(("arbitrary", "arbitrary")),
        name="rw_pre_prompt",
    )(proj, proj, proj, proj, *_rw_param_args(rw))


def _rw_pre_sample_body(zr_ref, zk_ref, zv_ref, zx_ref, sr_ref, sk_ref, sv_ref, sx_ref, *rest,
                        nb, t_new):
    prm = _rw_params(rest[:9])
    outs = rest[9:16]
    spread_refs = rest[16:20]
    zs = [zr_ref[...], zk_ref[...], zv_ref[...], zx_ref[...]]
    prevs = []
    for z, s_ref, e_ref in zip(zs, (sr_ref, sk_ref, sv_ref, sx_ref), spread_refs):
        e_ref[...] = jnp.zeros_like(e_ref)
        for cb in range(e_ref.shape[0]):
            e_ref[cb, pl.ds(0, nb, stride=t_new), :] = s_ref[:, cb * LANES:(cb + 1) * LANES]
        spread = jnp.concatenate([e_ref[cb] for cb in range(e_ref.shape[0])], axis=1)
        first = (_iota(z.shape, 0) % t_new) == 0
        prevs.append(jnp.where(first, spread, pltpu.roll(z, 1, 0)))
    res = _rw_prepare(*zs, *prevs, prm)
    for o, val in zip(outs, res):
        o[...] = val


def _rw_pre_sample(proj, shift0, rw, *, nb, t_new):
    w = RW_WIDTH
    tb = nb * t_new
    cols = _rw_pre_specs(None)
    in_specs = [pl.BlockSpec((tb, cw), functools.partial(lambda i, cb: (0, cb), cb=cb)) for cw, cb in cols]
    shifts = [shift0[:, 0:w], shift0[:, w:2 * w], shift0[:, 2 * w:3 * w], shift0[:, 3 * w:]]
    in_specs += [pl.BlockSpec(s.shape, lambda i: (0, 0)) for s in shifts]
    in_specs += _rw_param_specs(lambda i: (0, 0))
    out_spec = pl.BlockSpec((tb, w), lambda i: (0, 0))
    return pl.pallas_call(
        functools.partial(_rw_pre_sample_body, nb=nb, t_new=t_new),
        grid=(1,),
        in_specs=in_specs,
        out_specs=[out_spec] * 7,
        out_shape=[jax.ShapeDtypeStruct((tb, w), F32)] * 7,
        scratch_shapes=[pltpu.VMEM((cw // LANES, tb, LANES), F32) for cw, _ in cols],
        compiler_params=_cparams(("arbitrary",)),
        name="rw_pre_sample",
    )(proj, proj, proj, proj, *shifts, *_rw_param_args(rw))


RW_CHUNK = 64


def _rw_chunk_prep(units, masks):
    strict, incl, eye = masks
    r, lw_incl, lw_excl, l_tot, k, v, aa, bb = (list(x) for x in zip(*units))
    c = r[0].shape[0]
    bf = lambda x: x.astype(BF16)
    zero = jnp.zeros((), F32)
    at = _each(lambda a, l: bf(a * jnp.exp(l)), aa, lw_excl)
    rt = _each(lambda a, l: a * jnp.exp(l), r, lw_incl)
    en = _each(lambda l: jnp.exp(-l), lw_incl)
    kh = _each(lambda a, e: bf(a * e), k, en)
    bh = _each(lambda a, e: bf(a * e), bb, en)
    er = _each(lambda lt, l: jnp.exp(lt - l), l_tot, lw_incl)
    kb = _each(lambda a, e: bf(a * e), k, er)
    bbar = _each(lambda a, e: bf(a * e), bb, er)
    vb = _each(bf, v)
    ar = _each(lambda a, b: jnp.concatenate([a, bf(b)], axis=0), at, rt)
    mk = _each(_dot_nt, ar, kh)
    mb = _each(_dot_nt, ar, bh)
    m_ak = _each(lambda m: bf(jnp.where(strict, m[:c], zero)), mk)
    m_ab = _each(lambda m: jnp.where(strict, m[:c], zero), mb)
    m_rk = _each(lambda m: bf(jnp.where(incl, m[c:], zero)), mk)
    m_rb = _each(lambda m: bf(jnp.where(incl, m[c:], zero)), mb)
    p = m_ab
    t = _each(lambda m: jnp.where(eye, 1.0, zero) + m, m_ab)
    steps = max(1, (c - 1).bit_length()) - 1
    for _ in range(steps):
        pb = _each(bf, p)
        p = _each(_dot, pb, pb)
        t = _each(lambda t_, p_: t_ + _dot(bf(t_), bf(p_)), t, p)
    tb_ = _each(bf, t)
    mv = _each(lambda m, x: bf(_dot(m, x)), m_ak, vb)
    at2 = _each(lambda a, b: bf(_dot(a, b)), tb_, at)
    vp = _each(lambda a, b: bf(_dot(a, b)), tb_, mv)
    rp = _each(lambda a, m, x: bf(a + _dot(m, x)), rt, m_rb, at2)
    y0 = _each(lambda m1, x1, m2, x2: _dot(m1, x1) + _dot(m2, x2), m_rk, vb, m_rb, vp)
    g = _each(lambda a, b: bf(_dot_tn(a, b)), bbar, at2)
    hp = _each(lambda a, b, c_, d: _dot_tn(a, b) + _dot_tn(c_, d), vp, bbar, vb, kb)
    decay = _each(jnp.exp, l_tot)
    return list(zip(rp, y0, g, hp, decay))


def _rw_chunk_apply(prep, s0):
    rp, y0, g, hp, decay = prep
    s0b = s0.astype(BF16)
    return _dot_nt(rp, s0b) + y0, s0 * decay + _dot_nt(s0b, g) + hp


def _rw_scan_prompt_body(r_ref, lw_ref, k_ref, v_ref, aa_ref, bb_ref, y_ref, sout_ref, s_ref, *, tb):
    gi = pl.program_id(2)
    c = RW_CHUNK
    n = RW_HEAD_DIM
    hp = LANES // n

    @pl.when(gi == 0)
    def _():
        s_ref[...] = jnp.zeros_like(s_ref)

    row = _iota((c, c), 0)
    col = _iota((c, c), 1)
    masks = (col < row, col <= row, col == row)
    tri_incl = (col <= row).astype(BF16)

    units = []
    for ci in range(tb // c):
        rows = slice(ci * c, (ci + 1) * c)
        lw = lw_ref[rows, :]
        lw_incl = _dot_exact_rhs(tri_incl, lw)
        lw_excl = lw_incl - lw
        l_tot = lw_incl[c - 1:c, :]
        r, k, v, aa, bb = r_ref[rows, :], k_ref[rows, :], v_ref[rows, :], aa_ref[rows, :], bb_ref[rows, :]
        for hh in range(hp):
            sl = slice(hh * n, (hh + 1) * n)
            units.append((r[:, sl], lw_incl[:, sl], lw_excl[:, sl], l_tot[:, sl], k[:, sl],
                          v[:, sl], aa[:, sl], bb[:, sl]))
    preps = _rw_chunk_prep(units, masks)
    states = [s_ref[hh] for hh in range(hp)]
    for ci in range(tb // c):
        ys = []
        for hh in range(hp):
            y, states[hh] = _rw_chunk_apply(preps[ci * hp + hh], states[hh])
            ys.append(y)
        y_ref[ci * c:(ci + 1) * c, :] = jnp.concatenate(ys, axis=1)
    for hh in range(hp):
        s_ref[hh] = states[hh]
    sout_ref[0] = s_ref[...]


def _rw_scan_prompt(pre, *, batch, seq, tb):
    n_rows = batch * seq
    ng = seq // tb
    n = RW_HEAD_DIM
    hp = LANES // n
    npair = RW_WIDTH // LANES
    spec = pl.BlockSpec((tb, LANES), lambda b, p, g: (b * ng + g, p))
    return pl.pallas_call(
        functools.partial(_rw_scan_prompt_body, tb=tb),
        grid=(batch, npair, ng),
        in_specs=[spec] * 6,
        out_specs=[spec, pl.BlockSpec((1, hp, n, n), lambda b, p, g: (b, p, 0, 0))],
        out_shape=[jax.ShapeDtypeStruct((n_rows, RW_WIDTH), F32),
                   jax.ShapeDtypeStruct((batch, RW_HEADS, n, n), F32)],
        scratch_shapes=[pltpu.VMEM((hp, n, n), F32)],
        compiler_params=_cparams(("parallel", "parallel", "arbitrary")),
        name="rw_scan_prompt",
    )(*pre)


def _rw_scan_sample_body(s0_ref, r_ref, lw_ref, k_ref, v_ref, aa_ref, bb_ref,
                         y_ref, sout_ref, st_ref, xt_ref, yt_ref, *, nb, t_new):
    n = RW_HEAD_DIM
    hv = LANES
    st_ref[...] = s0_ref[...].T.reshape(hv, n, nb)
    srcs = (r_ref, lw_ref, k_ref, aa_ref, bb_ref, v_ref)
    for t in range(t_new):
        for vi, src in enumerate(srcs):
            x = src[pl.ds(t, nb, stride=t_new), :]
            if vi == 1:
                x = jnp.exp(x)
            xt_ref[vi, t] = x.T

    def row(i, carry):
        k0 = pl.multiple_of((i // n) * n, n)
        s = st_ref[i]
        for t in range(t_new):
            rr = xt_ref[0, t, pl.ds(k0, n), :]
            ww = xt_ref[1, t, pl.ds(k0, n), :]
            kk = xt_ref[2, t, pl.ds(k0, n), :]
            aa = xt_ref[3, t, pl.ds(k0, n), :]
            bb = xt_ref[4, t, pl.ds(k0, n), :]
            vv = xt_ref[5, t, pl.ds(i, 1), :]
            sa = jnp.sum(s * aa, axis=0, keepdims=True)
            s = s * ww + sa * bb + vv * kk
            yt_ref[t, pl.ds(i, 1), :] = jnp.sum(s * rr, axis=0, keepdims=True)
        st_ref[i] = s
        return carry

    lax.fori_loop(0, hv, row, 0)
    for t in range(t_new):
        y_ref[pl.ds(t, nb, stride=t_new), :] = yt_ref[t].T
    sout_ref[...] = st_ref[...].reshape(hv * n, nb).T


def _rw_scan_sample(pre, s0, *, nb, t_new):
    n = RW_HEAD_DIM
    tb = nb * t_new
    npair = RW_WIDTH // LANES
    sw = LANES * n
    s0f = s0.reshape(nb, RW_HEADS * n * n)
    spec = pl.BlockSpec((tb, LANES), lambda p: (0, p))
    sspec = pl.BlockSpec((nb, sw), lambda p: (0, p))
    y_new, s_new = pl.pallas_call(
        functools.partial(_rw_scan_sample_body, nb=nb, t_new=t_new),
        grid=(npair,),
        in_specs=[sspec] + [spec] * 6,
        out_specs=[spec, sspec],
        out_shape=[jax.ShapeDtypeStruct((tb, RW_WIDTH), F32), jax.ShapeDtypeStruct(s0f.shape, F32)],
        scratch_shapes=[pltpu.VMEM((LANES, n, nb), F32), pltpu.VMEM((6, t_new, LANES, nb), F32),
                        pltpu.VMEM((t_new, LANES, nb), F32)],
        compiler_params=_cparams(("arbitrary",)),
        name="rw_scan_sample",
    )(s0f, *pre)
    return y_new, s_new.reshape(s0.shape)


def _rw_post_body(y_ref, bonus_ref, g0, g1, g2, g3, gw_ref, gb_ref, seg_ref, o_ref):
    y = y_ref[...]
    seg = seg_ref[...]
    inv = 1.0 / RW_HEAD_DIM
    mean = _seg_sum(y, seg) * inv
    d = y - mean
    var = _seg_sum(d * d, seg) * inv
    yn = d * lax.rsqrt(var + RW_GN_EPS) * gw_ref[...] + gb_ref[...] + bonus_ref[...]
    gate = jnp.concatenate([g0[...], g1[...], g2[...], g3[...]], axis=1)
    o_ref[...] = (yn * _silu(gate)).astype(o_ref.dtype)


def _rw_post(y, bonus, proj, rw, *, tb):
    n_rows = y.shape[0]
    w = RW_WIDTH
    row = pl.BlockSpec((tb, w), lambda i: (i, 0))
    gspecs = [pl.BlockSpec((tb, LANES), functools.partial(lambda i, q: (i, OFF_RW_G // LANES + q), q=q))
              for q in range(w // LANES)]
    const = lambda shape: pl.BlockSpec(shape, lambda i: (0, 0))
    return pl.pallas_call(
        _rw_post_body,
        grid=(n_rows // tb,),
        in_specs=[row, row] + gspecs + [const((1, w)), const((1, w)), const((w, w))],
        out_specs=row,
        out_shape=jax.ShapeDtypeStruct((n_rows, w), BF16),
        compiler_params=_cparams(("parallel",)),
        name="rw_post",
    )(y, bonus, proj, proj, proj, proj, rw["gn_w"], rw["gn_b"], rw["seg"])


def _block_diag(w):
    nb, d, _ = w.shape
    eye = jnp.eye(nb, dtype=w.dtype)
    return (eye[:, None, :, None] * w[:, :, None, :]).reshape(nb * d, nb * d)


def _layer_tail(x, proj, osb, olru, pre, y_rw, p, lp, fg, final, nseq):
    orw = _rw_post(y_rw, pre[6], proj, lp["rw"], tb=lp["tm_in"])
    h = _outproj(x, osb, olru, orw, p, lp["w_out"], lp["ple_gate"], lp["ple_proj"], fg,
                 tm=lp["tm_out"], final=final)
    t = proj.shape[0] // nseq
    p3 = proj.reshape(nseq, t, D_IN_PROJ)
    heads = lambda off: p3[..., off:off + SB_WIDTH].reshape(nseq, t, SB_HEADS, SB_HEAD_DIM)
    hist = CONV_WIDTH - 1
    return h, (heads(OFF_SB_K), heads(OFF_SB_V), p3[:, t - hist:, OFF_LRU_X:OFF_LRU_G],
               p3[:, t - 1, OFF_RW:OFF_RW_G])


def _layer_prompt(x, p, lp, fg, *, batch, seq, final):
    proj = _inproj(x, lp["norm_g"], lp["w_in"], tm=lp["tm_in"], tn=lp["tn_in"])
    osb = _attn_prompt(proj, batch=batch, seq=seq, tq=lp["tq"])
    olru, h_last = _lru_prompt(proj, lp["lru"], batch=batch, seq=seq, tb=lp["lru_tb"])
    pre = _rw_pre_prompt(proj, lp["rw"], batch=batch, seq=seq, tb=lp["rw_tb"])
    y_rw, s_new = _rw_scan_prompt(pre[:6], batch=batch, seq=seq, tb=lp["rw_tb"])
    h, (k_new, v_new, conv_new, shift_new) = _layer_tail(x, proj, osb, olru, pre, y_rw, p, lp, fg, final, batch)
    return h, (k_new, v_new, h_last.reshape(batch, LRU_WIDTH), conv_new, s_new, shift_new)


def _layer_sample(x, p, cache_k, cache_v, page_table, lru_conv, lru_h, rw_shift, rw_s, lp, fg, *,
                  layer, dec_batch, t_new, final):
    proj = _inproj(x, lp["norm_g"], lp["w_in"], tm=lp["tm_in"], tn=lp["tn_in"])
    osb = _attn_sample(proj, cache_k, cache_v, page_table, layer=layer, dec_batch=dec_batch, t_new=t_new,
                       bb=lp["attn_bb"])
    olru, h_last = _lru_sample(proj, lru_conv, lru_h, lp["lru"], nb=dec_batch, t_new=t_new)
    pre = _rw_pre_sample(proj, rw_shift, lp["rw"], nb=dec_batch, t_new=t_new)
    y_rw, s_new = _rw_scan_sample(pre[:6], rw_s, nb=dec_batch, t_new=t_new)
    h, (k_new, v_new, conv_new, shift_new) = _layer_tail(x, proj, osb, olru, pre, y_rw, p, lp, fg, final,
                                                          dec_batch)
    return h, (k_new, v_new, h_last, conv_new, s_new, shift_new)


def _layer_params(l, norm_g, w_in, w_out, lru_conv_w, lru_conv_b, lru_gate_a_w, lru_gate_a_b,
                  lru_gate_x_w, lru_gate_x_b, lru_lambda, rw_mu, rw_w0, rw_w2, rw_a0, rw_a2,
                  rw_k_k, rw_k_a, rw_r_k, rw_gn_w, rw_gn_b, ple_proj, ple_gate):
    row = lambda a: a[l].reshape(1, -1)
    w = RW_WIDTH
    zeros = jnp.zeros((RW_LORA, w), F32)
    head_id = jnp.arange(w) // RW_HEAD_DIM
    lru = dict(conv_w=lru_conv_w[l], conv_b=row(lru_conv_b),
               gate_w=jnp.concatenate([_block_diag(lru_gate_a_w[l]), _block_diag(lru_gate_x_w[l])],
                                      axis=1).astype(BF16),
               gate_a_b=row(lru_gate_a_b), gate_x_b=row(lru_gate_x_b), lam=row(lru_lambda))
    rw = dict(mu=row(rw_mu), w0=row(rw_w0),
              w2=jnp.concatenate([rw_w2[l], zeros], axis=0).astype(BF16),
              a0=row(rw_a0),
              a2=jnp.concatenate([zeros, rw_a2[l]], axis=0).astype(BF16),
              k_k=row(rw_k_k), k_a=row(rw_k_a), r_k=row(rw_r_k),
              gn_w=row(rw_gn_w), gn_b=row(rw_gn_b),
              seg=(head_id[:, None] == head_id[None, :]).astype(BF16))
    return dict(norm_g=row(norm_g), w_in=w_in[l].astype(BF16), w_out=w_out[l].astype(BF16),
                ple_proj=ple_proj[l].astype(BF16), ple_gate=ple_gate[l].astype(BF16), lru=lru, rw=rw)


def _tiles(seq):
    return dict(tm_in=512, tn_in=D_IN_PROJ // 3, tq=128, attn_bb=4, lru_tb=min(512, seq),
                rw_tb=min(512, seq), tm_out=256)


def kernel(x_prompt, x_sample, p_prompt, p_sample, cache_sb_k, cache_sb_v, page_table, state_lru_h, state_lru_conv, state_rw_S, state_rw_shift, norm_g, w_in, w_out, lru_conv_w, lru_conv_b, lru_gate_a_w, lru_gate_a_b, lru_gate_x_w, lru_gate_x_b, lru_lambda, rw_mu, rw_w0, rw_w2, rw_a0, rw_a2, rw_k_k, rw_k_a, rw_r_k, rw_gn_w, rw_gn_b, ple_proj, ple_gate, final_norm_g):
    batch, seq, d = x_prompt.shape
    dec_batch, t_new, _ = x_sample.shape
    depth = w_in.shape[0]
    n_p, n_s = batch * seq, dec_batch * t_new
    xp = x_prompt.reshape(n_p, d)
    xs = x_sample.reshape(n_s, d)
    fg = final_norm_g.reshape(1, d)
    outs_p, outs_s = [], []
    for l in range(depth):
        lp = _layer_params(l, norm_g, w_in, w_out, lru_conv_w, lru_conv_b, lru_gate_a_w, lru_gate_a_b,
                           lru_gate_x_w, lru_gate_x_b, lru_lambda, rw_mu, rw_w0, rw_w2, rw_a0, rw_a2,
                           rw_k_k, rw_k_a, rw_r_k, rw_gn_w, rw_gn_b, ple_proj, ple_gate)
        lp.update(_tiles(seq))
        final = l == depth - 1
        xp, st_p = _layer_prompt(xp, p_prompt[l].reshape(n_p, -1), lp, fg, batch=batch, seq=seq, final=final)
        xs, st_s = _layer_sample(xs, p_sample[l].reshape(n_s, -1), cache_sb_k, cache_sb_v, page_table,
                                 state_lru_conv[l], state_lru_h[l], state_rw_shift[l], state_rw_S[l], lp, fg,
                                 layer=l, dec_batch=dec_batch, t_new=t_new, final=final)
        outs_p.append(st_p)
        outs_s.append(st_s)
    y_prompt = xp.reshape(batch, seq, d)
    y_sample = xs.reshape(dec_batch, t_new, d)
    stack = lambda outs, k: jnp.stack([o[k] for o in outs])
    return (y_prompt, y_sample, stack(outs_p, 0), stack(outs_p, 1), stack(outs_s, 0), stack(outs_s, 1),
            stack(outs_p, 2), stack(outs_s, 2), stack(outs_p, 3), stack(outs_s, 3),
            stack(outs_p, 4), stack(outs_s, 4), stack(outs_p, 5), stack(outs_s, 5))
```

```python
import functools
import math

import jax
import jax.numpy as jnp
from jax import lax
from jax.experimental import pallas as pl
from jax.experimental.pallas import tpu as pltpu

F32 = jnp.float32
BF16 = jnp.bfloat16

D_MODEL = 2048
D_PLE = 256
RMS_EPS = 1e-6
SB_HEADS = 8
SB_HEAD_DIM = 128
SB_WIDTH = SB_HEADS * SB_HEAD_DIM
LRU_WIDTH = 512
LRU_BLOCKS = 8
CONV_WIDTH = 4
LRU_C = 8.0
RW_WIDTH = 512
RW_HEAD_DIM = 64
RW_HEADS = 8
RW_LORA = 64
RW_GN_EPS = 64e-5
RW_SHIFT_WIDTH = 3 * RW_WIDTH + 2 * RW_LORA
OFF_SB_Q = 0
OFF_SB_K = OFF_SB_Q + SB_WIDTH
OFF_SB_V = OFF_SB_K + SB_WIDTH
OFF_SB_G = OFF_SB_V + SB_WIDTH
OFF_LRU_X = OFF_SB_G + SB_WIDTH
OFF_LRU_G = OFF_LRU_X + LRU_WIDTH
OFF_RW = OFF_LRU_G + LRU_WIDTH
OFF_RW_XWA = OFF_RW + 3 * RW_WIDTH
OFF_RW_G = OFF_RW + RW_SHIFT_WIDTH
D_IN_PROJ = OFF_RW_G + RW_WIDTH

LANES = 128
SUBLANES = 8
VMEM_LIMIT = 56 * 1024 * 1024

LOG_ZERO = -110.0
LOG_OFF = -1e30


def _cparams(sem, vmem=VMEM_LIMIT):
    return pltpu.CompilerParams(dimension_semantics=sem, vmem_limit_bytes=vmem)


def _dot(a, b):
    return jnp.dot(a, b, preferred_element_type=F32)


def _dot_nt(a, b):
    return lax.dot_general(a, b, (((1,), (1,)), ((), ())), preferred_element_type=F32)


def _dot_tn(a, b):
    return lax.dot_general(a, b, (((0,), (0,)), ((), ())), preferred_element_type=F32)


def _split3(x):
    x1 = x.astype(BF16)
    r1 = x - x1.astype(F32)
    x2 = r1.astype(BF16)
    x3 = (r1 - x2.astype(F32)).astype(BF16)
    return x1, x2, x3


def _dot_exact_lhs(x, m):
    x1, x2, x3 = _split3(x)
    return _dot(x1, m) + _dot(x2, m) + _dot(x3, m)


def _dot_hilo_lhs(x, m):
    rows = x.shape[0]
    hi = x.astype(BF16)
    lo = (x - hi.astype(F32)).astype(BF16)
    both = _dot(jnp.concatenate([hi, lo], axis=0), m)
    return both[:rows] + both[rows:]


def _dot_exact_rhs(m, x):
    x1, x2, x3 = _split3(x)
    return _dot(m, x1) + _dot(m, x2) + _dot(m, x3)


def _each(f, *lists):
    return [f(*xs) for xs in zip(*lists)]


def _iota(shape, dim):
    return lax.broadcasted_iota(jnp.int32, shape, dim)


def _silu(x):
    return x * jax.nn.sigmoid(x)


def _softplus(x):
    return jnp.maximum(x, 0.0) + jnp.log1p(jnp.exp(-jnp.abs(x)))


EXPM1_SERIES_TERMS = 7
EXPM1_SERIES_RANGE = 0.125


def _neg_expm1(x, ex):
    p = 1.0 + x * (1.0 / EXPM1_SERIES_TERMS)
    for n in range(EXPM1_SERIES_TERMS - 1, 1, -1):
        p = 1.0 + (x * (1.0 / n)) * p
    return jnp.where(x > -EXPM1_SERIES_RANGE, -x * p, 1.0 - ex)


def _log_sigmoid_pair(z):
    t = jnp.log(1.0 + jnp.exp(-jnp.abs(z)))
    return jnp.minimum(z, 0.0) - t, -jnp.maximum(z, 0.0) - t


def _inproj_body(x_ref, g_ref, w_ref, o_ref, u_ref):
    @pl.when(pl.program_id(1) == 0)
    def _():
        x = x_ref[...]
        ms = jnp.mean(x * x, axis=-1, keepdims=True)
        u_ref[...] = (x * lax.rsqrt(ms + RMS_EPS) * g_ref[...]).astype(BF16)

    o_ref[...] = _dot(u_ref[...], w_ref[...])


def _inproj(x, g, w, *, tm, tn):
    n, d = x.shape
    dout = w.shape[1]
    return pl.pallas_call(
        _inproj_body,
        grid=(n // tm, dout // tn),
        in_specs=[pl.BlockSpec((tm, d), lambda i, j: (i, 0)),
                  pl.BlockSpec((1, d), lambda i, j: (0, 0)),
                  pl.BlockSpec((d, tn), lambda i, j: (0, j))],
        out_specs=pl.BlockSpec((tm, tn), lambda i, j: (i, j)),
        out_shape=jax.ShapeDtypeStruct((n, dout), F32),
        scratch_shapes=[pltpu.VMEM((tm, d), BF16)],
        compiler_params=_cparams(("parallel", "arbitrary")),
        name="inproj",
    )(x, g, w)


def _outproj_body(x_ref, osb_ref, olru_ref, orw_ref, p_ref, w1_ref, w2_ref, w3_ref,
                  wg_ref, wp_ref, fg_ref, o_ref, *, final):
    h = x_ref[...] + _dot(osb_ref[...], w1_ref[...]) + _dot(olru_ref[...], w2_ref[...]) \
        + _dot(orw_ref[...], w3_ref[...])
    gate = _dot(h.astype(BF16), wg_ref[...])
    pp = _dot(p_ref[...].astype(BF16), wp_ref[...])
    h = h + jax.nn.sigmoid(gate) * pp
    if final:
        ms = jnp.mean(h * h, axis=-1, keepdims=True)
        h = h * lax.rsqrt(ms + RMS_EPS) * fg_ref[...]
    o_ref[...] = h


def _outproj(x, osb, olru, orw, p, w_out, w_gate, w_ple, fg, *, tm, final):
    n, d = x.shape
    wsb, wlru, wrw = osb.shape[1], olru.shape[1], orw.shape[1]
    dp = p.shape[1]
    row = lambda w: pl.BlockSpec((tm, w), lambda i: (i, 0))
    const = lambda shape, idx: pl.BlockSpec(shape, lambda i: idx)
    return pl.pallas_call(
        functools.partial(_outproj_body, final=final),
        grid=(n // tm,),
        in_specs=[row(d), row(wsb), row(wlru), row(wrw), row(dp),
                  const((wsb, d), (0, 0)),
                  const((wlru, d), (wsb // wlru, 0)),
                  const((wrw, d), ((wsb + wlru) // wrw, 0)),
                  const((d, d), (0, 0)),
                  const((dp, d), (0, 0)),
                  const((1, d), (0, 0))],
        out_specs=row(d),
        out_shape=jax.ShapeDtypeStruct((n, d), F32),
        compiler_params=_cparams(("parallel",)),
        name="outproj",
    )(x, osb, olru, orw, p, w_out, w_out, w_out, w_gate, w_ple, fg)


ATTN_NEAR_BLOCKS = 3


def _sb_block(qh, kf, vf, r_old, mask, tri, scale, log_gate=None):
    hd = SB_HEAD_DIM
    heads = range(len(qh))
    kh = [kf[:, h * hd:(h + 1) * hd].astype(BF16) for h in heads]
    vh = [vf[:, h * hd:(h + 1) * hd].astype(BF16) for h in heads]
    z = _each(lambda a, b: _dot_nt(a, b) * scale, qh, kh)
    ls = _each(_log_sigmoid_pair, z)
    lsz = [x[0] for x in ls]
    lstay = [x[1] for x in ls]
    if mask is not None:
        lstay = _each(lambda x: jnp.where(mask, x, 0.0), lstay)
    s_excl = _each(lambda x: _dot_hilo_lhs(x, tri), lstay)
    r_in = r_old if log_gate is None else _each(lambda r: r + log_gate, r_old)
    w = _each(lambda a, s, r: jnp.exp(a + s + r), lsz, s_excl, r_in)
    if mask is not None:
        w = _each(lambda x: jnp.where(mask, x, 0.0), w)
    pv = _each(lambda x, v: _dot(x.astype(BF16), v), w, vh)
    r_new = _each(lambda r, s, x: r + s[:, 0:1] + x[:, 0:1], r_old, s_excl, lstay)
    return pv, r_new


def _all_done(r_list):
    r_max = functools.reduce(jnp.maximum, r_list)
    return (jnp.max(r_max) < LOG_ZERO).astype(jnp.int32)


def _attn_prompt_body(*refs, tq, nq, scale):
    near = ATTN_NEAR_BLOCKS
    q_ref, g_ref = refs[0], refs[1]
    k_refs, v_refs = refs[2:2 + near], refs[2 + near:2 + 2 * near]
    proj_ref, o_ref, kbuf, vbuf, sem, acc_ref, r_ref = refs[2 + 2 * near:]
    b = pl.program_id(0)
    i = pl.program_id(1)
    hd = SB_HEAD_DIM
    heads = range(SB_HEADS)
    q = q_ref[...].astype(BF16)
    qh = [q[:, h * hd:(h + 1) * hd] for h in heads]
    row = _iota((tq, tq), 0)
    col = _iota((tq, tq), 1)
    tri = (row > col).astype(BF16)
    r = [jnp.zeros((tq, 1), F32) for _ in heads]
    acc = [jnp.zeros((tq, hd), F32) for _ in heads]
    for d in range(near):
        mask = (col < row) if d == 0 else None
        gate = None if d == 0 else jnp.where(i >= d, 0.0, LOG_OFF)
        pv, r = _sb_block(qh, k_refs[d][...], v_refs[d][...], r, mask, tri, scale, gate)
        acc = _each(lambda a, x: a + x, acc, pv)
    for h in heads:
        acc_ref[h] = acc[h]
        r_ref[h] = r[h]

    def copies(j):
        rows = pl.ds(pl.multiple_of((b * nq + j) * tq, tq), tq)
        return (pltpu.make_async_copy(proj_ref.at[rows, pl.ds(OFF_SB_K, SB_WIDTH)], kbuf, sem.at[0]),
                pltpu.make_async_copy(proj_ref.at[rows, pl.ds(OFF_SB_V, SB_WIDTH)], vbuf, sem.at[1]))

    def cond(c):
        j, done = c
        return jnp.logical_and(j >= 0, done == 0)

    def body(c):
        j, _ = c
        for cp in copies(j):
            cp.start()
        for cp in copies(j):
            cp.wait()
        pv, r_new = _sb_block(qh, kbuf[...], vbuf[...], [r_ref[h] for h in heads], None, tri, scale)
        for h in heads:
            acc_ref[h] += pv[h]
            r_ref[h] = r_new[h]
        return j - 1, _all_done(r_new)

    lax.while_loop(cond, body, (i - near, _all_done(r)))
    out = jnp.concatenate([acc_ref[h] for h in heads], axis=1)
    o_ref[...] = (out * _silu(g_ref[...])).astype(o_ref.dtype)


def _attn_prompt(proj, *, batch, seq, tq):
    n_rows = batch * seq
    nq = seq // tq
    w = SB_WIDTH
    scale = 1.0 / math.sqrt(SB_HEAD_DIM)
    near = ATTN_NEAR_BLOCKS
    cur = lambda off: pl.BlockSpec((tq, w), lambda b, i: (b * nq + i, off // w))
    back = lambda off, d: pl.BlockSpec((tq, w), lambda b, i: (b * nq + jnp.maximum(i - d, 0), off // w))
    return pl.pallas_call(
        functools.partial(_attn_prompt_body, tq=tq, nq=nq, scale=scale),
        grid=(batch, nq),
        in_specs=[cur(OFF_SB_Q), cur(OFF_SB_G)]
        + [back(OFF_SB_K, d) for d in range(near)] + [back(OFF_SB_V, d) for d in range(near)]
        + [pl.BlockSpec(memory_space=pl.ANY)],
        out_specs=pl.BlockSpec((tq, w), lambda b, i: (b * nq + i, 0)),
        out_shape=jax.ShapeDtypeStruct((n_rows, w), BF16),
        scratch_shapes=[pltpu.VMEM((tq, w), F32), pltpu.VMEM((tq, w), F32), pltpu.SemaphoreType.DMA((2,)),
                        pltpu.VMEM((SB_HEADS, tq, SB_HEAD_DIM), F32), pltpu.VMEM((SB_HEADS, tq, 1), F32)],
        compiler_params=_cparams(("parallel", "arbitrary")),
        name="attn_prompt",
    )(*([proj] * (2 + 2 * near + 1)))


ATTN_NEAR_PAGES = 2


def _sb_block_seqs(qh, kh, vh, r_old, mask, tri, scale):
    seqs = range(len(kh))
    heads = range(len(kh[0]))
    t = qh[0][0].shape[0]
    z = [[_dot_nt(qh[s][h], kh[s][h].astype(BF16)) for h in heads] for s in seqs]
    ls = [_log_sigmoid_pair(jnp.concatenate(z[s], axis=0) * scale) for s in seqs]
    lsz = [x[0] for x in ls]
    lstay = [x[1] for x in ls]
    if mask is not None:
        lstay = _each(lambda x: jnp.where(mask, x, 0.0), lstay)
    s_excl = _each(lambda x: _dot_hilo_lhs(x, tri), lstay)
    w = _each(lambda a, sx, r: jnp.exp(a + sx + r), lsz, s_excl, r_old)
    if mask is not None:
        w = _each(lambda x: jnp.where(mask, x, 0.0), w)
    pv = [[_dot(w[s][h * t:(h + 1) * t, :].astype(BF16), vh[s][h].astype(BF16)) for h in heads]
          for s in seqs]
    r_new = _each(lambda r, sx, x: r + sx[:, 0:1] + x[:, 0:1], r_old, s_excl, lstay)
    return pv, r_new


def _attn_sample_body(pt_ref, q_ref, kn_ref, vn_ref, g_ref, kc_ref, vc_ref, o_ref,
                      kbuf, vbuf, sem, acc_ref, r_ref, of_ref, *, layer, bb, t_new, n_pages, page, scale):
    i = pl.program_id(0)
    nsteps = pl.num_programs(0)
    heads = range(SB_HEADS)
    seqs = range(bb)
    hq = SB_HEADS * t_new
    hd = SB_HEAD_DIM
    near = min(ATTN_NEAR_PAGES, n_pages)
    extra_slot = 2 * bb * near
    par = i % 2

    def page_copies(b, j, slot):
        pg = pt_ref[b, j]
        return (pltpu.make_async_copy(kc_ref.at[layer, pg], kbuf.at[slot], sem.at[0, slot]),
                pltpu.make_async_copy(vc_ref.at[layer, pg], vbuf.at[slot], sem.at[1, slot]))

    def near_slot(parity, s, pi):
        return (parity * bb + s) * near + pi

    def near_copies(step, parity):
        return [c for s in seqs for pi in range(near)
                for c in page_copies(step * bb + s, n_pages - near + pi, near_slot(parity, s, pi))]

    @pl.when(i == 0)
    def _():
        for c in near_copies(i, par):
            c.start()

    @pl.when(i + 1 < nsteps)
    def _():
        for c in near_copies(i + 1, 1 - par):
            c.start()

    tri_new = (_iota((t_new, t_new), 0) > _iota((t_new, t_new), 1)).astype(BF16)
    npg = near * page
    tri_near = (_iota((npg, npg), 0) > _iota((npg, npg), 1)).astype(BF16)
    tri_page = tri_near[:page, :page]
    new_mask = _iota((hq, t_new), 1) < (_iota((hq, t_new), 0) % t_new)
    split = lambda x: [x[:, h * hd:(h + 1) * hd] for h in heads]
    head_rows = lambda buf, slot, h: buf[slot, pl.ds(h, page, stride=SB_HEADS), :]

    qh = [split(q_ref[s * t_new:(s + 1) * t_new, :].astype(BF16)) for s in seqs]
    r = [jnp.zeros((hq, 1), F32) for _ in seqs]
    kh = [split(kn_ref[s * t_new:(s + 1) * t_new, :]) for s in seqs]
    vh = [split(vn_ref[s * t_new:(s + 1) * t_new, :]) for s in seqs]
    acc, r = _sb_block_seqs(qh, kh, vh, r, new_mask, tri_new, scale)
    for c in near_copies(i, par):
        c.wait()
    page_rows = lambda buf, s, h: jnp.concatenate(
        [head_rows(buf, near_slot(par, s, pi), h) for pi in range(near)], axis=0)
    kh = [[page_rows(kbuf, s, h) for h in heads] for s in seqs]
    vh = [[page_rows(vbuf, s, h) for h in heads] for s in seqs]
    pv, r = _sb_block_seqs(qh, kh, vh, r, None, tri_near, scale)
    for s in seqs:
        for h in heads:
            acc_ref[s, h] = acc[s][h] + pv[s][h]
        r_ref[s] = r[s]

    for s in seqs:
        b = i * bb + s

        def cond(c):
            j, done = c
            return jnp.logical_and(j >= 0, done == 0)

        def body(c, b=b, s=s):
            j, _ = c
            for cp in page_copies(b, j, extra_slot):
                cp.start()
            for cp in page_copies(b, j, extra_slot):
                cp.wait()
            kj = [[head_rows(kbuf, extra_slot, h) for h in heads]]
            vj = [[head_rows(vbuf, extra_slot, h) for h in heads]]
            pv_j, r_j = _sb_block_seqs([qh[s]], kj, vj, [r_ref[s]], None, tri_page, scale)
            for h in heads:
                acc_ref[s, h] += pv_j[0][h]
            r_ref[s] = r_j[0]
            return j - 1, _all_done(r_j)

        lax.while_loop(cond, body, (jnp.int32(n_pages - near - 1), _all_done([r[s]])))
        o_b = jnp.concatenate([acc_ref[s, h] for h in heads], axis=1)
        rows = slice(s * t_new, (s + 1) * t_new)
        of_ref[rows, :] = o_b * _silu(g_ref[rows, :])
    o_ref[...] = of_ref[...].astype(o_ref.dtype)


def _attn_sample(proj, cache_k, cache_v, page_table, *, layer, dec_batch, t_new, bb):
    depth, n_phys, page = cache_k.shape[:3]
    n_pages = page_table.shape[1]
    tb = bb * t_new
    spec = lambda off: pl.BlockSpec((tb, SB_WIDTH), lambda i, pt: (i, off // SB_WIDTH))
    hq = SB_HEADS * t_new
    nslots = 2 * bb * min(ATTN_NEAR_PAGES, n_pages) + 1
    cache_k = cache_k.reshape(depth, n_phys, page * SB_HEADS, SB_HEAD_DIM)
    cache_v = cache_v.reshape(depth, n_phys, page * SB_HEADS, SB_HEAD_DIM)
    pagebuf = pltpu.VMEM((nslots, page * SB_HEADS, SB_HEAD_DIM), F32)
    grid_spec = pltpu.PrefetchScalarGridSpec(
        num_scalar_prefetch=1,
        grid=(dec_batch // bb,),
        in_specs=[spec(OFF_SB_Q), spec(OFF_SB_K), spec(OFF_SB_V), spec(OFF_SB_G),
                  pl.BlockSpec(memory_space=pl.ANY), pl.BlockSpec(memory_space=pl.ANY)],
        out_specs=pl.BlockSpec((tb, SB_WIDTH), lambda i, pt: (i, 0)),
        scratch_shapes=[pagebuf, pagebuf, pltpu.SemaphoreType.DMA((2, nslots)),
                        pltpu.VMEM((bb, SB_HEADS, t_new, SB_HEAD_DIM), F32), pltpu.VMEM((bb, hq, 1), F32),
                        pltpu.VMEM((tb, SB_WIDTH), F32)],
    )
    return pl.pallas_call(
        functools.partial(_attn_sample_body, layer=layer, bb=bb, t_new=t_new, n_pages=n_pages, page=page,
                          scale=1.0 / math.sqrt(SB_HEAD_DIM)),
        grid_spec=grid_spec,
        out_shape=jax.ShapeDtypeStruct((dec_batch * t_new, SB_WIDTH), BF16),
        compiler_params=_cparams(("arbitrary",)),
        name="attn_sample",
    )(page_table, proj, proj, proj, proj, cache_k, cache_v)


def _lru_gates(xc, gw_ref, ga_b_ref, gx_b_ref, lam_ref):
    w = xc.shape[-1]
    gates = _dot(xc.astype(BF16), gw_ref[...])
    r = jax.nn.sigmoid(gates[:, :w] + ga_b_ref[...])
    i = jax.nn.sigmoid(gates[:, w:] + gx_b_ref[...])
    log_a = -LRU_C * r * _softplus(-lam_ref[...])
    a = jnp.exp(log_a)
    mult = jnp.sqrt(_neg_expm1(2.0 * log_a, a * a))
    return a, mult * (i * xc)


def _lru_prompt_body(x_ref, g_ref, cw_ref, cb_ref, gw_ref, ga_b_ref, gx_b_ref, lam_ref,
                     o_ref, hl_ref, xp_ref, a_ref, b_ref, h_ref, *, tb):
    c = pl.program_id(1)
    pad = SUBLANES
    hist = CONV_WIDTH - 1

    @pl.when(c == 0)
    def _():
        xp_ref[0:pad, :] = jnp.zeros((pad, LRU_WIDTH), F32)
        h_ref[...] = jnp.zeros_like(h_ref)

    xb = x_ref[...]
    xp_ref[pad:pad + tb, :] = xb
    xc = cb_ref[...] + xp_ref[pad - hist:pad - hist + tb, :] * cw_ref[0:1, :]
    for j in range(1, CONV_WIDTH):
        xc = xc + xp_ref[pad - hist + j:pad - hist + j + tb, :] * cw_ref[j:j + 1, :]
    xp_ref[pad - hist:pad, :] = xb[tb - hist:tb, :]

    a, bt = _lru_gates(xc, gw_ref, ga_b_ref, gx_b_ref, lam_ref)
    rowg = _iota((tb, LRU_WIDTH), 0) % SUBLANES
    s = 1
    while s < SUBLANES:
        a_sh = pltpu.roll(a, s, 0)
        b_sh = pltpu.roll(bt, s, 0)
        ok = rowg >= s
        bt = jnp.where(ok, a * b_sh + bt, bt)
        a = jnp.where(ok, a * a_sh, a)
        s *= 2
    a_ref[...] = a
    b_ref[...] = bt

    def group(gi, h):
        r0 = pl.multiple_of(gi * SUBLANES, SUBLANES)
        hs = a_ref[pl.ds(r0, SUBLANES), :] * h + b_ref[pl.ds(r0, SUBLANES), :]
        b_ref[pl.ds(r0, SUBLANES), :] = hs
        return jnp.broadcast_to(hs[SUBLANES - 1:SUBLANES, :], (SUBLANES, LRU_WIDTH))

    h = lax.fori_loop(0, tb // SUBLANES, group, h_ref[...])
    h_ref[...] = h
    hl_ref[0] = h[0:1, :]
    o_ref[...] = (b_ref[...] * _silu(g_ref[...])).astype(o_ref.dtype)


def _lru_prompt(proj, lw, *, batch, seq, tb):
    n_rows = batch * seq
    nc = seq // tb
    w = LRU_WIDTH
    const = lambda shape: pl.BlockSpec(shape, lambda b, c: (0,) * len(shape))
    return pl.pallas_call(
        functools.partial(_lru_prompt_body, tb=tb),
        grid=(batch, nc),
        in_specs=[pl.BlockSpec((tb, w), lambda b, c: (b * nc + c, OFF_LRU_X // w)),
                  pl.BlockSpec((tb, w), lambda b, c: (b * nc + c, OFF_LRU_G // w)),
                  const((CONV_WIDTH, w)), const((1, w)), const((w, 2 * w)),
                  const((1, w)), const((1, w)), const((1, w))],
        out_specs=[pl.BlockSpec((tb, w), lambda b, c: (b * nc + c, 0)),
                   pl.BlockSpec((1, 1, w), lambda b, c: (b, 0, 0))],
        out_shape=[jax.ShapeDtypeStruct((n_rows, w), BF16),
                   jax.ShapeDtypeStruct((batch, 1, w), F32)],
        scratch_shapes=[pltpu.VMEM((SUBLANES + tb, w), F32), pltpu.VMEM((tb, w), F32),
                        pltpu.VMEM((tb, w), F32), pltpu.VMEM((SUBLANES, w), F32)],
        compiler_params=_cparams(("arbitrary", "arbitrary")),
        name="lru_prompt",
    )(proj, proj, lw["conv_w"], lw["conv_b"], lw["gate_w"], lw["gate_a_b"], lw["gate_x_b"], lw["lam"])


def _lru_sample_body(*refs, nb, t_new):
    ncb = LRU_WIDTH // LANES
    x_refs, cbuf_refs = refs[:ncb], refs[ncb:2 * ncb]
    (g_ref, h0_ref, cw_ref, cb_ref, gw_ref, ga_b_ref, gx_b_ref, lam_ref,
     o_ref, hl_ref, of_ref) = refs[2 * ncb:]
    hist = CONV_WIDTH - 1
    gather = lambda rs, start, stride: jnp.concatenate(
        [r[pl.ds(start, nb, stride=stride), :] for r in rs], axis=1)
    xs = [gather(cbuf_refs, j, hist) for j in range(hist)]
    xs += [gather(x_refs, t, t_new) for t in range(t_new)]
    h = h0_ref[...]
    for t in range(t_new):
        xc = cb_ref[...]
        for j in range(CONV_WIDTH):
            xc = xc + xs[t + j] * cw_ref[j:j + 1, :]
        a, bt = _lru_gates(xc, gw_ref, ga_b_ref, gx_b_ref, lam_ref)
        h = a * h + bt
        for cb in range(of_ref.shape[0]):
            of_ref[cb, pl.ds(t, nb, stride=t_new), :] = h[:, cb * LANES:(cb + 1) * LANES]
    hl_ref[...] = h
    hs = jnp.concatenate([of_ref[cb] for cb in range(of_ref.shape[0])], axis=1)
    o_ref[...] = (hs * _silu(g_ref[...])).astype(o_ref.dtype)


def _lru_sample(proj, conv_buf, h0, lw, *, nb, t_new):
    w = LRU_WIDTH
    tb = nb * t_new
    const = lambda shape: pl.BlockSpec(shape, lambda i: (0,) * len(shape))
    ncb = w // LANES
    hist = CONV_WIDTH - 1
    colblk = lambda rows, c: pl.BlockSpec((rows, LANES), lambda i: (0, c))
    return pl.pallas_call(
        functools.partial(_lru_sample_body, nb=nb, t_new=t_new),
        grid=(1,),
        in_specs=[colblk(tb, OFF_LRU_X // LANES + q) for q in range(ncb)]
        + [colblk(nb * hist, q) for q in range(ncb)]
        + [pl.BlockSpec((tb, w), lambda i: (0, OFF_LRU_G // w)), const((nb, w)),
           const((CONV_WIDTH, w)), const((1, w)), const((w, 2 * w)),
           const((1, w)), const((1, w)), const((1, w))],
        out_specs=[const((tb, w)), const((nb, w))],
        out_shape=[jax.ShapeDtypeStruct((tb, w), BF16),
                   jax.ShapeDtypeStruct((nb, w), F32)],
        scratch_shapes=[pltpu.VMEM((ncb, tb, LANES), F32)],
        compiler_params=_cparams(("arbitrary",)),
        name="lru_sample",
    )(*([proj] * ncb), *([conv_buf.reshape(nb * hist, w)] * ncb), proj, h0,
      lw["conv_w"], lw["conv_b"], lw["gate_w"], lw["gate_a_b"], lw["gate_x_b"], lw["lam"])


def _seg_sum(x, seg):
    return _dot_exact_lhs(x, seg)


def _rw_prepare(z_r, z_k, z_v, z_x, p_r, p_k, p_v, p_x, rp):
    mu = rp["mu"]
    w = RW_WIDTH
    mix = lambda z, p, lo, hi: z + mu[:, lo:hi] * (p - z)
    r = mix(z_r, p_r, 0, w)
    k = mix(z_k, p_k, w, 2 * w)
    v = mix(z_v, p_v, 2 * w, 3 * w)
    x = mix(z_x, p_x, 3 * w, 3 * w + 2 * RW_LORA)
    seg = rp["seg"]
    w_log = -_softplus(-(rp["w0"] + _dot(jnp.tanh(x).astype(BF16), rp["w2"]))) - 0.5
    logw = -jnp.exp(w_log)
    a = jax.nn.sigmoid(rp["a0"] + _dot(x.astype(BF16), rp["a2"]))
    kk = k * rp["k_k"]
    kk = kk * lax.rsqrt(_seg_sum(kk * kk, seg) + 1e-12)
    k2 = k * (1.0 + (a - 1.0) * rp["k_a"])
    bonus = _seg_sum(r * k2 * rp["r_k"], seg) * v
    return r, logw, k2, v, -kk, kk * a, bonus


def _rw_params(refs):
    names = ("mu", "w0", "w2", "a0", "a2", "k_k", "k_a", "r_k", "seg")
    return {n: r[...] for n, r in zip(names, refs)}


def _rw_param_specs(idx):
    w = RW_WIDTH
    shapes = [(1, RW_SHIFT_WIDTH), (1, w), (2 * RW_LORA, w), (1, w), (2 * RW_LORA, w),
              (1, w), (1, w), (1, w), (w, w)]
    return [pl.BlockSpec(s, idx) for s in shapes]


def _rw_param_args(rw):
    return [rw[n] for n in ("mu", "w0", "w2", "a0", "a2", "k_k", "k_a", "r_k", "seg")]


def _rw_pre_prompt_body(zr_ref, zk_ref, zv_ref, zx_ref, *rest, tb):
    prm = _rw_params(rest[:9])
    outs = rest[9:16]
    last = rest[16:20]
    c = pl.program_id(1)

    @pl.when(c == 0)
    def _():
        for l in last:
            l[...] = jnp.zeros_like(l)

    zs = [zr_ref[...], zk_ref[...], zv_ref[...], zx_ref[...]]
    prevs = []
    for z, l in zip(zs, last):
        first = _iota(z.shape, 0) == 0
        prevs.append(jnp.where(first, jnp.broadcast_to(l[0:1, :], z.shape), pltpu.roll(z, 1, 0)))
        l[...] = jnp.broadcast_to(z[tb - 1:tb, :], l.shape)
    res = _rw_prepare(*zs, *prevs, prm)
    for o, val in zip(outs, res):
        o[...] = val


def _rw_pre_specs(row_idx):
    w = RW_WIDTH
    xw = 2 * RW_LORA
    return [(w, OFF_RW // w), (w, OFF_RW // w + 1), (w, OFF_RW // w + 2), (xw, OFF_RW_XWA // xw)]


def _rw_pre_prompt(proj, rw, *, batch, seq, tb):
    n_rows = batch * seq
    nc = seq // tb
    w = RW_WIDTH
    cols = _rw_pre_specs(None)
    in_specs = [pl.BlockSpec((tb, cw), functools.partial(lambda b, c, cb: (b * nc + c, cb), cb=cb))
                for cw, cb in cols]
    in_specs += _rw_param_specs(lambda b, c: (0, 0))
    out_spec = pl.BlockSpec((tb, w), lambda b, c: (b * nc + c, 0))
    return pl.pallas_call(
        functools.partial(_rw_pre_prompt_body, tb=tb),
        grid=(batch, nc),
        in_specs=in_specs,
        out_specs=[out_spec] * 7,
        out_shape=[jax.ShapeDtypeStruct((n_rows, w), F32)] * 7,
        scratch_shapes=[pltpu.VMEM((SUBLANES, cw), F32) for cw, _ in cols],
        compiler_params=_cparams(("arbitrary", "arbitrary")),
        name="rw_pre_prompt",
    )(proj, proj, proj, proj, *_rw_param_args(rw))


def _rw_pre_sample_body(zr_ref, zk_ref, zv_ref, zx_ref, sr_ref, sk_ref, sv_ref, sx_ref, *rest,
                        nb, t_new):
    prm = _rw_params(rest[:9])
    outs = rest[9:16]
    spread_refs = rest[16:20]
    zs = [zr_ref[...], zk_ref[...], zv_ref[...], zx_ref[...]]
    prevs = []
    for z, s_ref, e_ref in zip(zs, (sr_ref, sk_ref, sv_ref, sx_ref), spread_refs):
        e_ref[...] = jnp.zeros_like(e_ref)
        for cb in range(e_ref.shape[0]):
            e_ref[cb, pl.ds(0, nb, stride=t_new), :] = s_ref[:, cb * LANES:(cb + 1) * LANES]
        spread = jnp.concatenate([e_ref[cb] for cb in range(e_ref.shape[0])], axis=1)
        first = (_iota(z.shape, 0) % t_new) == 0
        prevs.append(jnp.where(first, spread, pltpu.roll(z, 1, 0)))
    res = _rw_prepare(*zs, *prevs, prm)
    for o, val in zip(outs, res):
        o[...] = val


def _rw_pre_sample(proj, shift0, rw, *, nb, t_new):
    w = RW_WIDTH
    tb = nb * t_new
    cols = _rw_pre_specs(None)
    in_specs = [pl.BlockSpec((tb, cw), functools.partial(lambda i, cb: (0, cb), cb=cb)) for cw, cb in cols]
    shifts = [shift0[:, 0:w], shift0[:, w:2 * w], shift0[:, 2 * w:3 * w], shift0[:, 3 * w:]]
    in_specs += [pl.BlockSpec(s.shape, lambda i: (0, 0)) for s in shifts]
    in_specs += _rw_param_specs(lambda i: (0, 0))
    out_spec = pl.BlockSpec((tb, w), lambda i: (0, 0))
    return pl.pallas_call(
        functools.partial(_rw_pre_sample_body, nb=nb, t_new=t_new),
        grid=(1,),
        in_specs=in_specs,
        out_specs=[out_spec] * 7,
        out_shape=[jax.ShapeDtypeStruct((tb, w), F32)] * 7,
        scratch_shapes=[pltpu.VMEM((cw // LANES, tb, LANES), F32) for cw, _ in cols],
        compiler_params=_cparams(("arbitrary",)),
        name="rw_pre_sample",
    )(proj, proj, proj, proj, *shifts, *_rw_param_args(rw))


RW_CHUNK = 64


def _rw_chunk_prep(units, masks):
    strict, incl, eye = masks
    r, lw_incl, lw_excl, l_tot, k, v, aa, bb = (list(x) for x in zip(*units))
    c = r[0].shape[0]
    bf = lambda x: x.astype(BF16)
    zero = jnp.zeros((), F32)
    at = _each(lambda a, l: bf(a * jnp.exp(l)), aa, lw_excl)
    rt = _each(lambda a, l: a * jnp.exp(l), r, lw_incl)
    en = _each(lambda l: jnp.exp(-l), lw_incl)
    kh = _each(lambda a, e: bf(a * e), k, en)
    bh = _each(lambda a, e: bf(a * e), bb, en)
    er = _each(lambda lt, l: jnp.exp(lt - l), l_tot, lw_incl)
    kb = _each(lambda a, e: bf(a * e), k, er)
    bbar = _each(lambda a, e: bf(a * e), bb, er)
    vb = _each(bf, v)
    ar = _each(lambda a, b: jnp.concatenate([a, bf(b)], axis=0), at, rt)
    mk = _each(_dot_nt, ar, kh)
    mb = _each(_dot_nt, ar, bh)
    m_ak = _each(lambda m: bf(jnp.where(strict, m[:c], zero)), mk)
    m_ab = _each(lambda m: jnp.where(strict, m[:c], zero), mb)
    m_rk = _each(lambda m: bf(jnp.where(incl, m[c:], zero)), mk)
    m_rb = _each(lambda m: bf(jnp.where(incl, m[c:], zero)), mb)
    p = m_ab
    t = _each(lambda m: jnp.where(eye, 1.0, zero) + m, m_ab)
    steps = max(1, (c - 1).bit_length()) - 1
    for _ in range(steps):
        pb = _each(bf, p)
        p = _each(_dot, pb, pb)
        t = _each(lambda t_, p_: t_ + _dot(bf(t_), bf(p_)), t, p)
    tb_ = _each(bf, t)
    mv = _each(lambda m, x: bf(_dot(m, x)), m_ak, vb)
    at2 = _each(lambda a, b: bf(_dot(a, b)), tb_, at)
    vp = _each(lambda a, b: bf(_dot(a, b)), tb_, mv)
    rp = _each(lambda a, m, x: bf(a + _dot(m, x)), rt, m_rb, at2)
    y0 = _each(lambda m1, x1, m2, x2: _dot(m1, x1) + _dot(m2, x2), m_rk, vb, m_rb, vp)
    g = _each(lambda a, b: bf(_dot_tn(a, b)), bbar, at2)
    hp = _each(lambda a, b, c_, d: _dot_tn(a, b) + _dot_tn(c_, d), vp, bbar, vb, kb)
    decay = _each(jnp.exp, l_tot)
    return list(zip(rp, y0, g, hp, decay))


def _rw_chunk_apply(prep, s0):
    rp, y0, g, hp, decay = prep
    s0b = s0.astype(BF16)
    return _dot_nt(rp, s0b) + y0, s0 * decay + _dot_nt(s0b, g) + hp


def _rw_scan_prompt_body(r_ref, lw_ref, k_ref, v_ref, aa_ref, bb_ref, y_ref, sout_ref, s_ref, *, tb):
    gi = pl.program_id(2)
    c = RW_CHUNK
    n = RW_HEAD_DIM
    hp = LANES // n

    @pl.when(gi == 0)
    def _():
        s_ref[...] = jnp.zeros_like(s_ref)

    row = _iota((c, c), 0)
    col = _iota((c, c), 1)
    masks = (col < row, col <= row, col == row)
    tri_incl = (col <= row).astype(BF16)

    units = []
    for ci in range(tb // c):
        rows = slice(ci * c, (ci + 1) * c)
        lw = lw_ref[rows, :]
        lw_incl = _dot_exact_rhs(tri_incl, lw)
        lw_excl = lw_incl - lw
        l_tot = lw_incl[c - 1:c, :]
        r, k, v, aa, bb = r_ref[rows, :], k_ref[rows, :], v_ref[rows, :], aa_ref[rows, :], bb_ref[rows, :]
        for hh in range(hp):
            sl = slice(hh * n, (hh + 1) * n)
            units.append((r[:, sl], lw_incl[:, sl], lw_excl[:, sl], l_tot[:, sl], k[:, sl],
                          v[:, sl], aa[:, sl], bb[:, sl]))
    preps = _rw_chunk_prep(units, masks)
    states = [s_ref[hh] for hh in range(hp)]
    for ci in range(tb // c):
        ys = []
        for hh in range(hp):
            y, states[hh] = _rw_chunk_apply(preps[ci * hp + hh], states[hh])
            ys.append(y)
        y_ref[ci * c:(ci + 1) * c, :] = jnp.concatenate(ys, axis=1)
    for hh in range(hp):
        s_ref[hh] = states[hh]
    sout_ref[0] = s_ref[...]


def _rw_scan_prompt(pre, *, batch, seq, tb):
    n_rows = batch * seq
    ng = seq // tb
    n = RW_HEAD_DIM
    hp = LANES // n
    npair = RW_WIDTH // LANES
    spec = pl.BlockSpec((tb, LANES), lambda b, p, g: (b * ng + g, p))
    return pl.pallas_call(
        functools.partial(_rw_scan_prompt_body, tb=tb),
        grid=(batch, npair, ng),
        in_specs=[spec] * 6,
        out_specs=[spec, pl.BlockSpec((1, hp, n, n), lambda b, p, g: (b, p, 0, 0))],
        out_shape=[jax.ShapeDtypeStruct((n_rows, RW_WIDTH), F32),
                   jax.ShapeDtypeStruct((batch, RW_HEADS, n, n), F32)],
        scratch_shapes=[pltpu.VMEM((hp, n, n), F32)],
        compiler_params=_cparams(("parallel", "parallel", "arbitrary")),
        name="rw_scan_prompt",
    )(*pre)


RW_SAMPLE_ROWS_PER_ITER = 2


def _rw_scan_sample_body(s0_ref, r_ref, lw_ref, k_ref, v_ref, aa_ref, bb_ref,
                         y_ref, sout_ref, st_ref, xt_ref, yt_ref, *, nb, t_new):
    n = RW_HEAD_DIM
    hv = LANES
    st_ref[...] = s0_ref[...].T.reshape(hv, n, nb)
    srcs = (r_ref, lw_ref, k_ref, aa_ref, bb_ref, v_ref)
    for t in range(t_new):
        for vi, src in enumerate(srcs):
            x = src[pl.ds(t, nb, stride=t_new), :]
            if vi == 1:
                x = jnp.exp(x)
            xt_ref[vi, t] = x.T

    rows_per_iter = RW_SAMPLE_ROWS_PER_ITER

    def row_group(ig, carry):
        i0 = ig * rows_per_iter
        k0 = pl.multiple_of((i0 // n) * n, n)
        rows = [i0 + j for j in range(rows_per_iter)]
        s = [st_ref[i] for i in rows]
        for t in range(t_new):
            rr = xt_ref[0, t, pl.ds(k0, n), :]
            ww = xt_ref[1, t, pl.ds(k0, n), :]
            kk = xt_ref[2, t, pl.ds(k0, n), :]
            aa = xt_ref[3, t, pl.ds(k0, n), :]
            bb = xt_ref[4, t, pl.ds(k0, n), :]
            vv = [xt_ref[5, t, pl.ds(i, 1), :] for i in rows]
            sa = _each(lambda x: jnp.sum(x * aa, axis=0, keepdims=True), s)
            s = _each(lambda x, a, v: x * ww + a * bb + v * kk, s, sa, vv)
            ys = _each(lambda x: jnp.sum(x * rr, axis=0, keepdims=True), s)
            for i, y in zip(rows, ys):
                yt_ref[t, pl.ds(i, 1), :] = y
        for i, x in zip(rows, s):
            st_ref[i] = x
        return carry

    lax.fori_loop(0, hv // rows_per_iter, row_group, 0)
    for t in range(t_new):
        y_ref[pl.ds(t, nb, stride=t_new), :] = yt_ref[t].T
    sout_ref[...] = st_ref[...].reshape(hv * n, nb).T


def _rw_scan_sample(pre, s0, *, nb, t_new):
    n = RW_HEAD_DIM
    tb = nb * t_new
    npair = RW_WIDTH // LANES
    sw = LANES * n
    s0f = s0.reshape(nb, RW_HEADS * n * n)
    spec = pl.BlockSpec((tb, LANES), lambda p: (0, p))
    sspec = pl.BlockSpec((nb, sw), lambda p: (0, p))
    y_new, s_new = pl.pallas_call(
        functools.partial(_rw_scan_sample_body, nb=nb, t_new=t_new),
        grid=(npair,),
        in_specs=[sspec] + [spec] * 6,
        out_specs=[spec, sspec],
        out_shape=[jax.ShapeDtypeStruct((tb, RW_WIDTH), F32), jax.ShapeDtypeStruct(s0f.shape, F32)],
        scratch_shapes=[pltpu.VMEM((LANES, n, nb), F32), pltpu.VMEM((6, t_new, LANES, nb), F32),
                        pltpu.VMEM((t_new, LANES, nb), F32)],
        compiler_params=_cparams(("arbitrary",)),
        name="rw_scan_sample",
    )(s0f, *pre)
    return y_new, s_new.reshape(s0.shape)


def _rw_post_body(y_ref, bonus_ref, g0, g1, g2, g3, gw_ref, gb_ref, seg_ref, o_ref):
    y = y_ref[...]
    seg = seg_ref[...]
    inv = 1.0 / RW_HEAD_DIM
    mean = _seg_sum(y, seg) * inv
    d = y - mean
    var = _seg_sum(d * d, seg) * inv
    yn = d * lax.rsqrt(var + RW_GN_EPS) * gw_ref[...] + gb_ref[...] + bonus_ref[...]
    gate = jnp.concatenate([g0[...], g1[...], g2[...], g3[...]], axis=1)
    o_ref[...] = (yn * _silu(gate)).astype(o_ref.dtype)


def _rw_post(y, bonus, proj, rw, *, tb):
    n_rows = y.shape[0]
    w = RW_WIDTH
    row = pl.BlockSpec((tb, w), lambda i: (i, 0))
    gspecs = [pl.BlockSpec((tb, LANES), functools.partial(lambda i, q: (i, OFF_RW_G // LANES + q), q=q))
              for q in range(w // LANES)]
    const = lambda shape: pl.BlockSpec(shape, lambda i: (0, 0))
    return pl.pallas_call(
        _rw_post_body,
        grid=(n_rows // tb,),
        in_specs=[row, row] + gspecs + [const((1, w)), const((1, w)), const((w, w))],
        out_specs=row,
        out_shape=jax.ShapeDtypeStruct((n_rows, w), BF16),
        compiler_params=_cparams(("parallel",)),
        name="rw_post",
    )(y, bonus, proj, proj, proj, proj, rw["gn_w"], rw["gn_b"], rw["seg"])


def _block_diag(w):
    nb, d, _ = w.shape
    eye = jnp.eye(nb, dtype=w.dtype)
    return (eye[:, None, :, None] * w[:, :, None, :]).reshape(nb * d, nb * d)


def _layer_tail(x, proj, osb, olru, pre, y_rw, p, lp, fg, final, nseq):
    orw = _rw_post(y_rw, pre[6], proj, lp["rw"], tb=lp["tm_in"])
    h = _outproj(x, osb, olru, orw, p, lp["w_out"], lp["ple_gate"], lp["ple_proj"], fg,
                 tm=lp["tm_out"], final=final)
    t = proj.shape[0] // nseq
    p3 = proj.reshape(nseq, t, D_IN_PROJ)
    hist = CONV_WIDTH - 1
    return h, (proj, proj, p3[:, t - hist:, OFF_LRU_X:OFF_LRU_G], p3[:, t - 1, OFF_RW:OFF_RW_G])


def _kv_new_body(*refs, depth):
    ins, (ko_ref, vo_ref) = refs[:2 * depth], refs[2 * depth:]
    rows = ins[0].shape[0]
    for l in range(depth):
        for src, dst in ((ins[2 * l], ko_ref), (ins[2 * l + 1], vo_ref)):
            x = src[...]
            for h in range(SB_HEADS):
                dst[l, pl.ds(h, rows, stride=SB_HEADS), :] = x[:, h * SB_HEAD_DIM:(h + 1) * SB_HEAD_DIM]


def _kv_new(projs, *, tb):
    depth = len(projs)
    rows = projs[0].shape[0]
    col = lambda off: pl.BlockSpec((tb, SB_WIDTH), lambda i: (i, off // SB_WIDTH))
    out_spec = pl.BlockSpec((depth, tb * SB_HEADS, SB_HEAD_DIM), lambda i: (0, i, 0))
    out_shape = jax.ShapeDtypeStruct((depth, rows * SB_HEADS, SB_HEAD_DIM), F32)
    return pl.pallas_call(
        functools.partial(_kv_new_body, depth=depth),
        grid=(rows // tb,),
        in_specs=[col(OFF_SB_K), col(OFF_SB_V)] * depth,
        out_specs=[out_spec, out_spec],
        out_shape=[out_shape, out_shape],
        compiler_params=_cparams(("parallel",)),
        name="kv_new",
    )(*[p for proj in projs for p in (proj, proj)])


def _layer_prompt(x, p, lp, fg, *, batch, seq, final):
    proj = _inproj(x, lp["norm_g"], lp["w_in"], tm=lp["tm_in"], tn=lp["tn_in"])
    osb = _attn_prompt(proj, batch=batch, seq=seq, tq=lp["tq"])
    olru, h_last = _lru_prompt(proj, lp["lru"], batch=batch, seq=seq, tb=lp["lru_tb"])
    pre = _rw_pre_prompt(proj, lp["rw"], batch=batch, seq=seq, tb=lp["rw_tb"])
    y_rw, s_new = _rw_scan_prompt(pre[:6], batch=batch, seq=seq, tb=lp["rw_tb"])
    h, (k_new, v_new, conv_new, shift_new) = _layer_tail(x, proj, osb, olru, pre, y_rw, p, lp, fg, final, batch)
    return h, (k_new, v_new, h_last.reshape(batch, LRU_WIDTH), conv_new, s_new, shift_new)


def _layer_sample(x, p, cache_k, cache_v, page_table, lru_conv, lru_h, rw_shift, rw_s, lp, fg, *,
                  layer, dec_batch, t_new, final):
    proj = _inproj(x, lp["norm_g"], lp["w_in"], tm=lp["tm_in"], tn=lp["tn_in"])
    osb = _attn_sample(proj, cache_k, cache_v, page_table, layer=layer, dec_batch=dec_batch, t_new=t_new,
                       bb=lp["attn_bb"])
    olru, h_last = _lru_sample(proj, lru_conv, lru_h, lp["lru"], nb=dec_batch, t_new=t_new)
    pre = _rw_pre_sample(proj, rw_shift, lp["rw"], nb=dec_batch, t_new=t_new)
    y_rw, s_new = _rw_scan_sample(pre[:6], rw_s, nb=dec_batch, t_new=t_new)
    h, (k_new, v_new, conv_new, shift_new) = _layer_tail(x, proj, osb, olru, pre, y_rw, p, lp, fg, final,
                                                          dec_batch)
    return h, (k_new, v_new, h_last, conv_new, s_new, shift_new)


def _layer_params(l, norm_g, w_in, w_out, lru_conv_w, lru_conv_b, lru_gate_a_w, lru_gate_a_b,
                  lru_gate_x_w, lru_gate_x_b, lru_lambda, rw_mu, rw_w0, rw_w2, rw_a0, rw_a2,
                  rw_k_k, rw_k_a, rw_r_k, rw_gn_w, rw_gn_b, ple_proj, ple_gate):
    row = lambda a: a[l].reshape(1, -1)
    w = RW_WIDTH
    zeros = jnp.zeros((RW_LORA, w), F32)
    head_id = jnp.arange(w) // RW_HEAD_DIM
    lru = dict(conv_w=lru_conv_w[l], conv_b=row(lru_conv_b),
               gate_w=jnp.concatenate([_block_diag(lru_gate_a_w[l]), _block_diag(lru_gate_x_w[l])],
                                      axis=1).astype(BF16),
               gate_a_b=row(lru_gate_a_b), gate_x_b=row(lru_gate_x_b), lam=row(lru_lambda))
    rw = dict(mu=row(rw_mu), w0=row(rw_w0),
              w2=jnp.concatenate([rw_w2[l], zeros], axis=0).astype(BF16),
              a0=row(rw_a0),
              a2=jnp.concatenate([zeros, rw_a2[l]], axis=0).astype(BF16),
              k_k=row(rw_k_k), k_a=row(rw_k_a), r_k=row(rw_r_k),
              gn_w=row(rw_gn_w), gn_b=row(rw_gn_b),
              seg=(head_id[:, None] == head_id[None, :]).astype(BF16))
    return dict(norm_g=row(norm_g), w_in=w_in[l].astype(BF16), w_out=w_out[l].astype(BF16),
                ple_proj=ple_proj[l].astype(BF16), ple_gate=ple_gate[l].astype(BF16), lru=lru, rw=rw)


def _tiles(seq):
    return dict(tm_in=512, tn_in=D_IN_PROJ // 3, tq=128, attn_bb=4, lru_tb=min(512, seq),
                rw_tb=min(512, seq), tm_out=256)


def kernel(x_prompt, x_sample, p_prompt, p_sample, cache_sb_k, cache_sb_v, page_table, state_lru_h, state_lru_conv, state_rw_S, state_rw_shift, norm_g, w_in, w_out, lru_conv_w, lru_conv_b, lru_gate_a_w, lru_gate_a_b, lru_gate_x_w, lru_gate_x_b, lru_lambda, rw_mu, rw_w0, rw_w2, rw_a0, rw_a2, rw_k_k, rw_k_a, rw_r_k, rw_gn_w, rw_gn_b, ple_proj, ple_gate, final_norm_g):
    batch, seq, d = x_prompt.shape
    dec_batch, t_new, _ = x_sample.shape
    depth = w_in.shape[0]
    n_p, n_s = batch * seq, dec_batch * t_new
    xp = x_prompt.reshape(n_p, d)
    xs = x_sample.reshape(n_s, d)
    fg = final_norm_g.reshape(1, d)
    outs_p, outs_s = [], []
    for l in range(depth):
        lp = _layer_params(l, norm_g, w_in, w_out, lru_conv_w, lru_conv_b, lru_gate_a_w, lru_gate_a_b,
                           lru_gate_x_w, lru_gate_x_b, lru_lambda, rw_mu, rw_w0, rw_w2, rw_a0, rw_a2,
                           rw_k_k, rw_k_a, rw_r_k, rw_gn_w, rw_gn_b, ple_proj, ple_gate)
        lp.update(_tiles(seq))
        final = l == depth - 1
        xp, st_p = _layer_prompt(xp, p_prompt[l].reshape(n_p, -1), lp, fg, batch=batch, seq=seq, final=final)
        xs, st_s = _layer_sample(xs, p_sample[l].reshape(n_s, -1), cache_sb_k, cache_sb_v, page_table,
                                 state_lru_conv[l], state_lru_h[l], state_rw_shift[l], state_rw_S[l], lp, fg,
                                 layer=l, dec_batch=dec_batch, t_new=t_new, final=final)
        outs_p.append(st_p)
        outs_s.append(st_s)
    y_prompt = xp.reshape(batch, seq, d)
    y_sample = xs.reshape(dec_batch, t_new, d)
    stack = lambda outs, k: jnp.stack([o[k] for o in outs])
    tiles = _tiles(seq)
    k_p, v_p = _kv_new([o[0] for o in outs_p], tb=tiles["tm_in"])
    k_s, v_s = _kv_new([o[0] for o in outs_s], tb=tiles["tm_in"])
    heads_p = lambda a: a.reshape(depth, batch, seq, SB_HEADS, SB_HEAD_DIM)
    heads_s = lambda a: a.reshape(depth, dec_batch, t_new, SB_HEADS, SB_HEAD_DIM)
    return (y_prompt, y_sample, heads_p(k_p), heads_p(v_p), heads_s(k_s), heads_s(v_s),
            stack(outs_p, 2), stack(outs_s, 2), stack(outs_p, 3), stack(outs_s, 3),
            stack(outs_p, 4), stack(outs_s, 4), stack(outs_p, 5), stack(outs_s, 5))
```

```python
import functools
import math

import jax
import jax.numpy as jnp
from jax import lax
from jax.experimental import pallas as pl
from jax.experimental.pallas import tpu as pltpu

F32 = jnp.float32
BF16 = jnp.bfloat16

D_MODEL = 2048
D_PLE = 256
RMS_EPS = 1e-6
SB_HEADS = 8
SB_HEAD_DIM = 128
SB_WIDTH = SB_HEADS * SB_HEAD_DIM
LRU_WIDTH = 512
LRU_BLOCKS = 8
CONV_WIDTH = 4
LRU_C = 8.0
RW_WIDTH = 512
RW_HEAD_DIM = 64
RW_HEADS = 8
RW_LORA = 64
RW_GN_EPS = 64e-5
RW_SHIFT_WIDTH = 3 * RW_WIDTH + 2 * RW_LORA
OFF_SB_Q = 0
OFF_SB_K = OFF_SB_Q + SB_WIDTH
OFF_SB_V = OFF_SB_K + SB_WIDTH
OFF_SB_G = OFF_SB_V + SB_WIDTH
OFF_LRU_X = OFF_SB_G + SB_WIDTH
OFF_LRU_G = OFF_LRU_X + LRU_WIDTH
OFF_RW = OFF_LRU_G + LRU_WIDTH
OFF_RW_XWA = OFF_RW + 3 * RW_WIDTH
OFF_RW_G = OFF_RW + RW_SHIFT_WIDTH
D_IN_PROJ = OFF_RW_G + RW_WIDTH

LANES = 128
SUBLANES = 8
VMEM_LIMIT = 56 * 1024 * 1024

LOG_ZERO = -110.0
LOG_OFF = -1e30


def _cparams(sem, vmem=VMEM_LIMIT):
    return pltpu.CompilerParams(dimension_semantics=sem, vmem_limit_bytes=vmem)


def _dot(a, b):
    return jnp.dot(a, b, preferred_element_type=F32)


def _dot_nt(a, b):
    return lax.dot_general(a, b, (((1,), (1,)), ((), ())), preferred_element_type=F32)


def _dot_tn(a, b):
    return lax.dot_general(a, b, (((0,), (0,)), ((), ())), preferred_element_type=F32)


def _dot_hilo_lhs(x, m):
    rows = x.shape[0]
    hi = x.astype(BF16)
    lo = (x - hi.astype(F32)).astype(BF16)
    both = _dot(jnp.concatenate([hi, lo], axis=0), m)
    return both[:rows] + both[rows:]


def _dot_hilo_rhs(m, x):
    cols = x.shape[1]
    hi = x.astype(BF16)
    lo = (x - hi.astype(F32)).astype(BF16)
    both = _dot(m, jnp.concatenate([hi, lo], axis=1))
    return both[:, :cols] + both[:, cols:]


def _each(f, *lists):
    return [f(*xs) for xs in zip(*lists)]


def _iota(shape, dim):
    return lax.broadcasted_iota(jnp.int32, shape, dim)


def _silu(x):
    return x * jax.nn.sigmoid(x)


def _softplus(x):
    return jnp.maximum(x, 0.0) + jnp.log1p(jnp.exp(-jnp.abs(x)))


EXPM1_SERIES_TERMS = 7
EXPM1_SERIES_RANGE = 0.125


def _neg_expm1(x, ex):
    p = 1.0 + x * (1.0 / EXPM1_SERIES_TERMS)
    for n in range(EXPM1_SERIES_TERMS - 1, 1, -1):
        p = 1.0 + (x * (1.0 / n)) * p
    return jnp.where(x > -EXPM1_SERIES_RANGE, -x * p, 1.0 - ex)


def _log_sigmoid_pair(z):
    t = jnp.log(1.0 + jnp.exp(-jnp.abs(z)))
    return jnp.minimum(z, 0.0) - t, -jnp.maximum(z, 0.0) - t


def _inproj_body(x_ref, g_ref, w_ref, o_ref, u_ref):
    @pl.when(pl.program_id(1) == 0)
    def _():
        x = x_ref[...]
        ms = jnp.mean(x * x, axis=-1, keepdims=True)
        u_ref[...] = (x * lax.rsqrt(ms + RMS_EPS) * g_ref[...]).astype(BF16)

    o_ref[...] = _dot(u_ref[...], w_ref[...])


def _inproj(x, g, w, *, layer, tm, tn):
    n, d = x.shape
    dout = w.shape[2]
    return pl.pallas_call(
        _inproj_body,
        grid=(n // tm, dout // tn),
        in_specs=[pl.BlockSpec((tm, d), lambda i, j: (i, 0)),
                  pl.BlockSpec((1, d), lambda i, j: (0, 0)),
                  pl.BlockSpec((None, d, tn), lambda i, j: (layer, 0, j))],
        out_specs=pl.BlockSpec((tm, tn), lambda i, j: (i, j)),
        out_shape=jax.ShapeDtypeStruct((n, dout), F32),
        scratch_shapes=[pltpu.VMEM((tm, d), BF16)],
        compiler_params=_cparams(("parallel", "arbitrary")),
        name="inproj",
    )(x, g, w)


def _outproj_body(x_ref, osb_ref, olru_ref, orw_ref, p_ref, w1_ref, w2_ref, w3_ref,
                  wg_ref, wp_ref, fg_ref, o_ref, *, final):
    h = x_ref[...] + _dot(osb_ref[...], w1_ref[...]) + _dot(olru_ref[...], w2_ref[...]) \
        + _dot(orw_ref[...], w3_ref[...])
    gate = _dot(h.astype(BF16), wg_ref[...])
    pp = _dot(p_ref[...].astype(BF16), wp_ref[...])
    h = h + jax.nn.sigmoid(gate) * pp
    if final:
        ms = jnp.mean(h * h, axis=-1, keepdims=True)
        h = h * lax.rsqrt(ms + RMS_EPS) * fg_ref[...]
    o_ref[...] = h


def _outproj(x, osb, olru, orw, p, w_out, w_gate, w_ple, fg, *, layer, tm, final):
    n, d = x.shape
    wsb, wlru, wrw = osb.shape[1], olru.shape[1], orw.shape[1]
    dp = p.shape[2]
    row = lambda w: pl.BlockSpec((tm, w), lambda i: (i, 0))
    const = lambda shape, idx: pl.BlockSpec((None,) + shape, lambda i: (layer,) + idx)
    return pl.pallas_call(
        functools.partial(_outproj_body, final=final),
        grid=(n // tm,),
        in_specs=[row(d), row(wsb), row(wlru), row(wrw),
                  pl.BlockSpec((None, tm, dp), lambda i: (layer, i, 0)),
                  const((wsb, d), (0, 0)),
                  const((wlru, d), (wsb // wlru, 0)),
                  const((wrw, d), ((wsb + wlru) // wrw, 0)),
                  const((d, d), (0, 0)),
                  const((dp, d), (0, 0)),
                  pl.BlockSpec((1, d), lambda i: (0, 0))],
        out_specs=row(d),
        out_shape=jax.ShapeDtypeStruct((n, d), F32),
        compiler_params=_cparams(("parallel",)),
        name="outproj",
    )(x, osb, olru, orw, p, w_out, w_out, w_out, w_gate, w_ple, fg)


ATTN_NEAR_BLOCKS = 3


def _sb_block(qh, kf, vf, r_old, mask, tri, scale, log_gate=None):
    hd = SB_HEAD_DIM
    heads = range(len(qh))
    kh = [kf[:, h * hd:(h + 1) * hd].astype(BF16) for h in heads]
    vh = [vf[:, h * hd:(h + 1) * hd].astype(BF16) for h in heads]
    z = _each(lambda a, b: _dot_nt(a, b) * scale, qh, kh)
    ls = _each(_log_sigmoid_pair, z)
    lsz = [x[0] for x in ls]
    lstay = [x[1] for x in ls]
    if mask is not None:
        lstay = _each(lambda x: jnp.where(mask, x, 0.0), lstay)
    s_excl = _each(lambda x: _dot_hilo_lhs(x, tri), lstay)
    r_in = r_old if log_gate is None else _each(lambda r: r + log_gate, r_old)
    w = _each(lambda a, s, r: jnp.exp(a + s + r), lsz, s_excl, r_in)
    if mask is not None:
        w = _each(lambda x: jnp.where(mask, x, 0.0), w)
    pv = _each(lambda x, v: _dot(x.astype(BF16), v), w, vh)
    r_new = _each(lambda r, s, x: r + s[:, 0:1] + x[:, 0:1], r_old, s_excl, lstay)
    return pv, r_new


def _all_done(r_list):
    r_max = functools.reduce(jnp.maximum, r_list)
    return (jnp.max(r_max) < LOG_ZERO).astype(jnp.int32)


def _attn_prompt_body(*refs, tq, nq, scale):
    near = ATTN_NEAR_BLOCKS
    q_ref, g_ref = refs[0], refs[1]
    k_refs, v_refs = refs[2:2 + near], refs[2 + near:2 + 2 * near]
    proj_ref, o_ref, kbuf, vbuf, sem, acc_ref, r_ref = refs[2 + 2 * near:]
    b = pl.program_id(0)
    i = pl.program_id(1)
    hd = SB_HEAD_DIM
    heads = range(SB_HEADS)
    q = q_ref[...].astype(BF16)
    qh = [q[:, h * hd:(h + 1) * hd] for h in heads]
    row = _iota((tq, tq), 0)
    col = _iota((tq, tq), 1)
    tri = (row > col).astype(BF16)
    r = [jnp.zeros((tq, 1), F32) for _ in heads]
    acc = [jnp.zeros((tq, hd), F32) for _ in heads]
    for d in range(near):
        mask = (col < row) if d == 0 else None
        gate = None if d == 0 else jnp.where(i >= d, 0.0, LOG_OFF)
        pv, r = _sb_block(qh, k_refs[d][...], v_refs[d][...], r, mask, tri, scale, gate)
        acc = _each(lambda a, x: a + x, acc, pv)
    for h in heads:
        acc_ref[h] = acc[h]
        r_ref[h] = r[h]

    def copies(j):
        rows = pl.ds(pl.multiple_of((b * nq + j) * tq, tq), tq)
        return (pltpu.make_async_copy(proj_ref.at[rows, pl.ds(OFF_SB_K, SB_WIDTH)], kbuf, sem.at[0]),
                pltpu.make_async_copy(proj_ref.at[rows, pl.ds(OFF_SB_V, SB_WIDTH)], vbuf, sem.at[1]))

    def cond(c):
        j, done = c
        return jnp.logical_and(j >= 0, done == 0)

    def body(c):
        j, _ = c
        for cp in copies(j):
            cp.start()
        for cp in copies(j):
            cp.wait()
        pv, r_new = _sb_block(qh, kbuf[...], vbuf[...], [r_ref[h] for h in heads], None, tri, scale)
        for h in heads:
            acc_ref[h] += pv[h]
            r_ref[h] = r_new[h]
        return j - 1, _all_done(r_new)

    lax.while_loop(cond, body, (i - near, _all_done(r)))
    out = jnp.concatenate([acc_ref[h] for h in heads], axis=1)
    o_ref[...] = (out * _silu(g_ref[...])).astype(o_ref.dtype)


def _attn_prompt(proj, *, batch, seq, tq):
    n_rows = batch * seq
    nq = seq // tq
    w = SB_WIDTH
    scale = 1.0 / math.sqrt(SB_HEAD_DIM)
    near = ATTN_NEAR_BLOCKS
    cur = lambda off: pl.BlockSpec((tq, w), lambda b, i: (b * nq + i, off // w))
    back = lambda off, d: pl.BlockSpec((tq, w), lambda b, i: (b * nq + jnp.maximum(i - d, 0), off // w))
    return pl.pallas_call(
        functools.partial(_attn_prompt_body, tq=tq, nq=nq, scale=scale),
        grid=(batch, nq),
        in_specs=[cur(OFF_SB_Q), cur(OFF_SB_G)]
        + [back(OFF_SB_K, d) for d in range(near)] + [back(OFF_SB_V, d) for d in range(near)]
        + [pl.BlockSpec(memory_space=pl.ANY)],
        out_specs=pl.BlockSpec((tq, w), lambda b, i: (b * nq + i, 0)),
        out_shape=jax.ShapeDtypeStruct((n_rows, w), BF16),
        scratch_shapes=[pltpu.VMEM((tq, w), F32), pltpu.VMEM((tq, w), F32), pltpu.SemaphoreType.DMA((2,)),
                        pltpu.VMEM((SB_HEADS, tq, SB_HEAD_DIM), F32), pltpu.VMEM((SB_HEADS, tq, 1), F32)],
        compiler_params=_cparams(("parallel", "arbitrary")),
        name="attn_prompt",
    )(*([proj] * (2 + 2 * near + 1)))


ATTN_NEAR_PAGES = 2


def _sb_block_seqs(qh, kh, vh, r_old, mask, tri, scale):
    seqs = range(len(kh))
    heads = range(len(kh[0]))
    t = qh[0][0].shape[0]
    z = [[_dot_nt(qh[s][h], kh[s][h].astype(BF16)) for h in heads] for s in seqs]
    ls = [_log_sigmoid_pair(jnp.concatenate(z[s], axis=0) * scale) for s in seqs]
    lsz = [x[0] for x in ls]
    lstay = [x[1] for x in ls]
    if mask is not None:
        lstay = _each(lambda x: jnp.where(mask, x, 0.0), lstay)
    s_excl = _each(lambda x: _dot_hilo_lhs(x, tri), lstay)
    w = _each(lambda a, sx, r: jnp.exp(a + sx + r), lsz, s_excl, r_old)
    if mask is not None:
        w = _each(lambda x: jnp.where(mask, x, 0.0), w)
    pv = [[_dot(w[s][h * t:(h + 1) * t, :].astype(BF16), vh[s][h].astype(BF16)) for h in heads]
          for s in seqs]
    r_new = _each(lambda r, sx, x: r + sx[:, 0:1] + x[:, 0:1], r_old, s_excl, lstay)
    return pv, r_new


def _attn_sample_body(pt_ref, q_ref, kn_ref, vn_ref, g_ref, kc_ref, vc_ref, o_ref,
                      kbuf, vbuf, sem, acc_ref, r_ref, of_ref, *, layer, bb, t_new, n_pages, page, scale):
    i = pl.program_id(0)
    nsteps = pl.num_programs(0)
    heads = range(SB_HEADS)
    seqs = range(bb)
    hq = SB_HEADS * t_new
    hd = SB_HEAD_DIM
    near = min(ATTN_NEAR_PAGES, n_pages)
    extra_slot = 2 * bb * near
    par = i % 2

    def page_copies(b, j, slot):
        pg = pt_ref[b, j]
        return (pltpu.make_async_copy(kc_ref.at[layer, pg], kbuf.at[slot], sem.at[0, slot]),
                pltpu.make_async_copy(vc_ref.at[layer, pg], vbuf.at[slot], sem.at[1, slot]))

    def near_slot(parity, s, pi):
        return (parity * bb + s) * near + pi

    def near_copies(step, parity):
        return [c for s in seqs for pi in range(near)
                for c in page_copies(step * bb + s, n_pages - near + pi, near_slot(parity, s, pi))]

    @pl.when(i == 0)
    def _():
        for c in near_copies(i, par):
            c.start()

    @pl.when(i + 1 < nsteps)
    def _():
        for c in near_copies(i + 1, 1 - par):
            c.start()

    tri_new = (_iota((t_new, t_new), 0) > _iota((t_new, t_new), 1)).astype(BF16)
    npg = near * page
    tri_near = (_iota((npg, npg), 0) > _iota((npg, npg), 1)).astype(BF16)
    tri_page = tri_near[:page, :page]
    new_mask = _iota((hq, t_new), 1) < (_iota((hq, t_new), 0) % t_new)
    split = lambda x: [x[:, h * hd:(h + 1) * hd] for h in heads]
    head_rows = lambda buf, slot, h: buf[slot, pl.ds(h, page, stride=SB_HEADS), :]

    qh = [split(q_ref[s * t_new:(s + 1) * t_new, :].astype(BF16)) for s in seqs]
    r = [jnp.zeros((hq, 1), F32) for _ in seqs]
    kh = [split(kn_ref[s * t_new:(s + 1) * t_new, :]) for s in seqs]
    vh = [split(vn_ref[s * t_new:(s + 1) * t_new, :]) for s in seqs]
    acc, r = _sb_block_seqs(qh, kh, vh, r, new_mask, tri_new, scale)
    for c in near_copies(i, par):
        c.wait()
    page_rows = lambda buf, s, h: jnp.concatenate(
        [head_rows(buf, near_slot(par, s, pi), h) for pi in range(near)], axis=0)
    kh = [[page_rows(kbuf, s, h) for h in heads] for s in seqs]
    vh = [[page_rows(vbuf, s, h) for h in heads] for s in seqs]
    pv, r = _sb_block_seqs(qh, kh, vh, r, None, tri_near, scale)
    for s in seqs:
        for h in heads:
            acc_ref[s, h] = acc[s][h] + pv[s][h]
        r_ref[s] = r[s]

    for s in seqs:
        b = i * bb + s

        def cond(c):
            j, done = c
            return jnp.logical_and(j >= 0, done == 0)

        def body(c, b=b, s=s):
            j, _ = c
            for cp in page_copies(b, j, extra_slot):
                cp.start()
            for cp in page_copies(b, j, extra_slot):
                cp.wait()
            kj = [[head_rows(kbuf, extra_slot, h) for h in heads]]
            vj = [[head_rows(vbuf, extra_slot, h) for h in heads]]
            pv_j, r_j = _sb_block_seqs([qh[s]], kj, vj, [r_ref[s]], None, tri_page, scale)
            for h in heads:
                acc_ref[s, h] += pv_j[0][h]
            r_ref[s] = r_j[0]
            return j - 1, _all_done(r_j)

        lax.while_loop(cond, body, (jnp.int32(n_pages - near - 1), _all_done([r[s]])))
        o_b = jnp.concatenate([acc_ref[s, h] for h in heads], axis=1)
        rows = slice(s * t_new, (s + 1) * t_new)
        of_ref[rows, :] = o_b * _silu(g_ref[rows, :])
    o_ref[...] = of_ref[...].astype(o_ref.dtype)


def _attn_sample(proj, cache_k, cache_v, page_table, *, layer, dec_batch, t_new, bb):
    depth, n_phys, page = cache_k.shape[:3]
    n_pages = page_table.shape[1]
    tb = bb * t_new
    spec = lambda off: pl.BlockSpec((tb, SB_WIDTH), lambda i, pt: (i, off // SB_WIDTH))
    hq = SB_HEADS * t_new
    nslots = 2 * bb * min(ATTN_NEAR_PAGES, n_pages) + 1
    cache_k = cache_k.reshape(depth, n_phys, page * SB_HEADS, SB_HEAD_DIM)
    cache_v = cache_v.reshape(depth, n_phys, page * SB_HEADS, SB_HEAD_DIM)
    pagebuf = pltpu.VMEM((nslots, page * SB_HEADS, SB_HEAD_DIM), F32)
    grid_spec = pltpu.PrefetchScalarGridSpec(
        num_scalar_prefetch=1,
        grid=(dec_batch // bb,),
        in_specs=[spec(OFF_SB_Q), spec(OFF_SB_K), spec(OFF_SB_V), spec(OFF_SB_G),
                  pl.BlockSpec(memory_space=pl.ANY), pl.BlockSpec(memory_space=pl.ANY)],
        out_specs=pl.BlockSpec((tb, SB_WIDTH), lambda i, pt: (i, 0)),
        scratch_shapes=[pagebuf, pagebuf, pltpu.SemaphoreType.DMA((2, nslots)),
                        pltpu.VMEM((bb, SB_HEADS, t_new, SB_HEAD_DIM), F32), pltpu.VMEM((bb, hq, 1), F32),
                        pltpu.VMEM((tb, SB_WIDTH), F32)],
    )
    return pl.pallas_call(
        functools.partial(_attn_sample_body, layer=layer, bb=bb, t_new=t_new, n_pages=n_pages, page=page,
                          scale=1.0 / math.sqrt(SB_HEAD_DIM)),
        grid_spec=grid_spec,
        out_shape=jax.ShapeDtypeStruct((dec_batch * t_new, SB_WIDTH), BF16),
        compiler_params=_cparams(("arbitrary",)),
        name="attn_sample",
    )(page_table, proj, proj, proj, proj, cache_k, cache_v)


def _lru_gates(xc, gw_ref, ga_b_ref, gx_b_ref, lam_ref):
    w = xc.shape[-1]
    gates = _dot(xc.astype(BF16), gw_ref[...])
    r = jax.nn.sigmoid(gates[:, :w] + ga_b_ref[...])
    i = jax.nn.sigmoid(gates[:, w:] + gx_b_ref[...])
    log_a = -LRU_C * r * _softplus(-lam_ref[...])
    a = jnp.exp(log_a)
    mult = jnp.sqrt(_neg_expm1(2.0 * log_a, a * a))
    return a, mult * (i * xc)


def _lru_prompt_body(x_ref, g_ref, cw_ref, cb_ref, gw_ref, ga_b_ref, gx_b_ref, lam_ref,
                     o_ref, hl_ref, xp_ref, a_ref, b_ref, h_ref, *, tb):
    c = pl.program_id(1)
    pad = SUBLANES
    hist = CONV_WIDTH - 1

    @pl.when(c == 0)
    def _():
        xp_ref[0:pad, :] = jnp.zeros((pad, LRU_WIDTH), F32)
        h_ref[...] = jnp.zeros_like(h_ref)

    xb = x_ref[...]
    xp_ref[pad:pad + tb, :] = xb
    xc = cb_ref[...] + xp_ref[pad - hist:pad - hist + tb, :] * cw_ref[0:1, :]
    for j in range(1, CONV_WIDTH):
        xc = xc + xp_ref[pad - hist + j:pad - hist + j + tb, :] * cw_ref[j:j + 1, :]
    xp_ref[pad - hist:pad, :] = xb[tb - hist:tb, :]

    a, bt = _lru_gates(xc, gw_ref, ga_b_ref, gx_b_ref, lam_ref)
    rowg = _iota((tb, LRU_WIDTH), 0) % SUBLANES
    s = 1
    while s < SUBLANES:
        a_sh = pltpu.roll(a, s, 0)
        b_sh = pltpu.roll(bt, s, 0)
        ok = rowg >= s
        bt = jnp.where(ok, a * b_sh + bt, bt)
        a = jnp.where(ok, a * a_sh, a)
        s *= 2
    a_ref[...] = a
    b_ref[...] = bt

    def group(gi, h):
        r0 = pl.multiple_of(gi * SUBLANES, SUBLANES)
        hs = a_ref[pl.ds(r0, SUBLANES), :] * h + b_ref[pl.ds(r0, SUBLANES), :]
        b_ref[pl.ds(r0, SUBLANES), :] = hs
        return jnp.broadcast_to(hs[SUBLANES - 1:SUBLANES, :], (SUBLANES, LRU_WIDTH))

    h = lax.fori_loop(0, tb // SUBLANES, group, h_ref[...])
    h_ref[...] = h
    hl_ref[0] = h[0:1, :]
    o_ref[...] = (b_ref[...] * _silu(g_ref[...])).astype(o_ref.dtype)


def _lru_prompt(proj, lw, *, batch, seq, tb):
    n_rows = batch * seq
    nc = seq // tb
    w = LRU_WIDTH
    const = lambda shape: pl.BlockSpec(shape, lambda b, c: (0,) * len(shape))
    return pl.pallas_call(
        functools.partial(_lru_prompt_body, tb=tb),
        grid=(batch, nc),
        in_specs=[pl.BlockSpec((tb, w), lambda b, c: (b * nc + c, OFF_LRU_X // w)),
                  pl.BlockSpec((tb, w), lambda b, c: (b * nc + c, OFF_LRU_G // w)),
                  const((CONV_WIDTH, w)), const((1, w)), const((w, 2 * w)),
                  const((1, w)), const((1, w)), const((1, w))],
        out_specs=[pl.BlockSpec((tb, w), lambda b, c: (b * nc + c, 0)),
                   pl.BlockSpec((1, 1, w), lambda b, c: (b, 0, 0))],
        out_shape=[jax.ShapeDtypeStruct((n_rows, w), BF16),
                   jax.ShapeDtypeStruct((batch, 1, w), F32)],
        scratch_shapes=[pltpu.VMEM((SUBLANES + tb, w), F32), pltpu.VMEM((tb, w), F32),
                        pltpu.VMEM((tb, w), F32), pltpu.VMEM((SUBLANES, w), F32)],
        compiler_params=_cparams(("arbitrary", "arbitrary")),
        name="lru_prompt",
    )(proj, proj, lw["conv_w"], lw["conv_b"], lw["gate_w"], lw["gate_a_b"], lw["gate_x_b"], lw["lam"])


def _lru_sample_body(*refs, nb, t_new):
    ncb = LRU_WIDTH // LANES
    x_refs, cbuf_refs = refs[:ncb], refs[ncb:2 * ncb]
    (g_ref, h0_ref, cw_ref, cb_ref, gw_ref, ga_b_ref, gx_b_ref, lam_ref,
     o_ref, hl_ref, of_ref) = refs[2 * ncb:]
    hist = CONV_WIDTH - 1
    gather = lambda rs, start, stride: jnp.concatenate(
        [r[pl.ds(start, nb, stride=stride), :] for r in rs], axis=1)
    xs = [gather(cbuf_refs, j, hist) for j in range(hist)]
    xs += [gather(x_refs, t, t_new) for t in range(t_new)]
    h = h0_ref[...]
    for t in range(t_new):
        xc = cb_ref[...]
        for j in range(CONV_WIDTH):
            xc = xc + xs[t + j] * cw_ref[j:j + 1, :]
        a, bt = _lru_gates(xc, gw_ref, ga_b_ref, gx_b_ref, lam_ref)
        h = a * h + bt
        for cb in range(of_ref.shape[0]):
            of_ref[cb, pl.ds(t, nb, stride=t_new), :] = h[:, cb * LANES:(cb + 1) * LANES]
    hl_ref[...] = h
    hs = jnp.concatenate([of_ref[cb] for cb in range(of_ref.shape[0])], axis=1)
    o_ref[...] = (hs * _silu(g_ref[...])).astype(o_ref.dtype)


def _lru_sample(proj, conv_buf, h0, lw, *, nb, t_new):
    w = LRU_WIDTH
    tb = nb * t_new
    const = lambda shape: pl.BlockSpec(shape, lambda i: (0,) * len(shape))
    ncb = w // LANES
    hist = CONV_WIDTH - 1
    colblk = lambda rows, c: pl.BlockSpec((rows, LANES), lambda i: (0, c))
    return pl.pallas_call(
        functools.partial(_lru_sample_body, nb=nb, t_new=t_new),
        grid=(1,),
        in_specs=[colblk(tb, OFF_LRU_X // LANES + q) for q in range(ncb)]
        + [colblk(nb * hist, q) for q in range(ncb)]
        + [pl.BlockSpec((tb, w), lambda i: (0, OFF_LRU_G // w)), const((nb, w)),
           const((CONV_WIDTH, w)), const((1, w)), const((w, 2 * w)),
           const((1, w)), const((1, w)), const((1, w))],
        out_specs=[const((tb, w)), const((nb, w))],
        out_shape=[jax.ShapeDtypeStruct((tb, w), BF16),
                   jax.ShapeDtypeStruct((nb, w), F32)],
        scratch_shapes=[pltpu.VMEM((ncb, tb, LANES), F32)],
        compiler_params=_cparams(("arbitrary",)),
        name="lru_sample",
    )(*([proj] * ncb), *([conv_buf.reshape(nb * hist, w)] * ncb), proj, h0,
      lw["conv_w"], lw["conv_b"], lw["gate_w"], lw["gate_a_b"], lw["gate_x_b"], lw["lam"])


def _seg_sum(x, seg):
    return _dot_hilo_lhs(x, seg)


def _rw_prepare(z_r, z_k, z_v, z_x, p_r, p_k, p_v, p_x, rp):
    mix = lambda z, p, mu: z + mu * (p - z)
    r = mix(z_r, p_r, rp["mu_r"])
    k = mix(z_k, p_k, rp["mu_k"])
    v = mix(z_v, p_v, rp["mu_v"])
    x = mix(z_x, p_x, rp["mu_x"])
    seg = rp["seg"]
    w_lin = rp["w0"] + _dot(jnp.tanh(x).astype(BF16), rp["w2"])
    w_log = _log_sigmoid_pair(w_lin)[0] - 0.5
    logw = -jnp.exp(w_log)
    a = jax.nn.sigmoid(rp["a0"] + _dot(x.astype(BF16), rp["a2"]))
    kk = k * rp["k_k"]
    kk = kk * lax.rsqrt(_seg_sum(kk * kk, seg) + 1e-12)
    k2 = k * (1.0 + (a - 1.0) * rp["k_a"])
    bonus = _seg_sum(r * k2 * rp["r_k"], seg) * v
    return r, logw, k2, v, -kk, kk * a, bonus


def _rw_params(refs):
    names = ("mu", "w0", "w2", "a0", "a2", "k_k", "k_a", "r_k", "seg")
    prm = {n: r[...] for n, r in zip(names, refs)}
    w = RW_WIDTH
    mu = prm.pop("mu")
    prm.update(mu_r=mu[:, 0:w], mu_k=mu[:, w:2 * w], mu_v=mu[:, 2 * w:3 * w], mu_x=mu[:, 3 * w:])
    return prm


def _rw_param_specs(idx):
    w = RW_WIDTH
    shapes = [(1, RW_SHIFT_WIDTH), (1, w), (2 * RW_LORA, w), (1, w), (2 * RW_LORA, w),
              (1, w), (1, w), (1, w), (w, w)]
    return [pl.BlockSpec(s, idx) for s in shapes]


def _rw_param_args(rw):
    return [rw[n] for n in ("mu", "w0", "w2", "a0", "a2", "k_k", "k_a", "r_k", "seg")]


def _rw_pre_specs(row_idx):
    w = RW_WIDTH
    xw = 2 * RW_LORA
    return [(w, OFF_RW // w), (w, OFF_RW // w + 1), (w, OFF_RW // w + 2), (xw, OFF_RW_XWA // xw)]


def _rw_pre_sample_body(zr_ref, zk_ref, zv_ref, zx_ref, sr_ref, sk_ref, sv_ref, sx_ref, *rest,
                        nb, t_new):
    prm = _rw_params(rest[:9])
    outs = rest[9:16]
    spread_refs = rest[16:20]
    zs = [zr_ref[...], zk_ref[...], zv_ref[...], zx_ref[...]]
    prevs = []
    for z, s_ref, e_ref in zip(zs, (sr_ref, sk_ref, sv_ref, sx_ref), spread_refs):
        e_ref[...] = jnp.zeros_like(e_ref)
        for cb in range(e_ref.shape[0]):
            e_ref[cb, pl.ds(0, nb, stride=t_new), :] = s_ref[:, cb * LANES:(cb + 1) * LANES]
        spread = jnp.concatenate([e_ref[cb] for cb in range(e_ref.shape[0])], axis=1)
        first = (_iota(z.shape, 0) % t_new) == 0
        prevs.append(jnp.where(first, spread, pltpu.roll(z, 1, 0)))
    res = _rw_prepare(*zs, *prevs, prm)
    for o, val in zip(outs, res):
        o[...] = val


def _rw_pre_sample(proj, shift0, rw, *, nb, t_new):
    w = RW_WIDTH
    tb = nb * t_new
    cols = _rw_pre_specs(None)
    in_specs = [pl.BlockSpec((tb, cw), functools.partial(lambda i, cb: (0, cb), cb=cb)) for cw, cb in cols]
    shifts = [shift0[:, 0:w], shift0[:, w:2 * w], shift0[:, 2 * w:3 * w], shift0[:, 3 * w:]]
    in_specs += [pl.BlockSpec(s.shape, lambda i: (0, 0)) for s in shifts]
    in_specs += _rw_param_specs(lambda i: (0, 0))
    out_spec = pl.BlockSpec((tb, w), lambda i: (0, 0))
    return pl.pallas_call(
        functools.partial(_rw_pre_sample_body, nb=nb, t_new=t_new),
        grid=(1,),
        in_specs=in_specs,
        out_specs=[out_spec] * 7,
        out_shape=[jax.ShapeDtypeStruct((tb, w), F32)] * 7,
        scratch_shapes=[pltpu.VMEM((cw // LANES, tb, LANES), F32) for cw, _ in cols],
        compiler_params=_cparams(("arbitrary",)),
        name="rw_pre_sample",
    )(proj, proj, proj, proj, *shifts, *_rw_param_args(rw))


RW_CHUNK = 64


def _rw_chunk_prep(units, masks):
    strict, incl, eye = masks
    r, lw_incl, lw_excl, l_tot, k, v, aa, bb = (list(x) for x in zip(*units))
    c = r[0].shape[0]
    bf = lambda x: x.astype(BF16)
    zero = jnp.zeros((), F32)
    at = _each(lambda a, l: bf(a * jnp.exp(l)), aa, lw_excl)
    rt = _each(lambda a, l: a * jnp.exp(l), r, lw_incl)
    en = _each(lambda l: jnp.exp(-l), lw_incl)
    kh = _each(lambda a, e: bf(a * e), k, en)
    bh = _each(lambda a, e: bf(a * e), bb, en)
    er = _each(lambda lt, l: jnp.exp(lt - l), l_tot, lw_incl)
    kb = _each(lambda a, e: bf(a * e), k, er)
    bbar = _each(lambda a, e: bf(a * e), bb, er)
    vb = _each(bf, v)
    ar = _each(lambda a, b: jnp.concatenate([a, bf(b)], axis=0), at, rt)
    mk = _each(_dot_nt, ar, kh)
    mb = _each(_dot_nt, ar, bh)
    m_ak = _each(lambda m: bf(jnp.where(strict, m[:c], zero)), mk)
    m_ab = _each(lambda m: jnp.where(strict, m[:c], zero), mb)
    m_rk = _each(lambda m: bf(jnp.where(incl, m[c:], zero)), mk)
    m_rb = _each(lambda m: bf(jnp.where(incl, m[c:], zero)), mb)
    p = m_ab
    t = _each(lambda m: jnp.where(eye, 1.0, zero) + m, m_ab)
    steps = max(1, (c - 1).bit_length()) - 1
    for _ in range(steps):
        pb = _each(bf, p)
        p = _each(_dot, pb, pb)
        t = _each(lambda t_, p_: t_ + _dot(bf(t_), bf(p_)), t, p)
    tb_ = _each(bf, t)
    mv = _each(lambda m, x: bf(_dot(m, x)), m_ak, vb)
    at2 = _each(lambda a, b: bf(_dot(a, b)), tb_, at)
    vp = _each(lambda a, b: bf(_dot(a, b)), tb_, mv)
    rp = _each(lambda a, m, x: bf(a + _dot(m, x)), rt, m_rb, at2)
    y0 = _each(lambda m1, x1, m2, x2: _dot(m1, x1) + _dot(m2, x2), m_rk, vb, m_rb, vp)
    g = _each(lambda a, b: bf(_dot_tn(a, b)), bbar, at2)
    hp = _each(lambda a, b, c_, d: _dot_tn(a, b) + _dot_tn(c_, d), vp, bbar, vb, kb)
    decay = _each(jnp.exp, l_tot)
    return list(zip(rp, y0, g, hp, decay))


def _rw_chunk_apply(prep, s0):
    rp, y0, g, hp, decay = prep
    s0b = s0.astype(BF16)
    return _dot_nt(rp, s0b) + y0, s0 * decay + _dot_nt(s0b, g) + hp


def _rw_chunked_scan(r, lw, k, v, aa, bb, s_ref):
    c = RW_CHUNK
    n = RW_HEAD_DIM
    hp = LANES // n
    nchunk = r.shape[0] // c
    row = _iota((c, c), 0)
    col = _iota((c, c), 1)
    masks = (col < row, col <= row, col == row)
    tri_incl = (col <= row).astype(BF16)

    units = []
    for ci in range(nchunk):
        rows = slice(ci * c, (ci + 1) * c)
        lw_c = lw[rows, :]
        lw_incl = _dot_hilo_rhs(tri_incl, lw_c)
        lw_excl = lw_incl - lw_c
        l_tot = lw_incl[c - 1:c, :]
        for hh in range(hp):
            sl = slice(hh * n, (hh + 1) * n)
            units.append((r[rows, sl], lw_incl[:, sl], lw_excl[:, sl], l_tot[:, sl], k[rows, sl],
                          v[rows, sl], aa[rows, sl], bb[rows, sl]))
    preps = _rw_chunk_prep(units, masks)
    states = [s_ref[hh] for hh in range(hp)]
    ys = []
    for ci in range(nchunk):
        y_h = []
        for hh in range(hp):
            y, states[hh] = _rw_chunk_apply(preps[ci * hp + hh], states[hh])
            y_h.append(y)
        ys.append(jnp.concatenate(y_h, axis=1))
    for hh in range(hp):
        s_ref[hh] = states[hh]
    return jnp.concatenate(ys, axis=0)


def _rw_finish(y, bonus, gate, gn_w, gn_b, seg):
    inv = 1.0 / RW_HEAD_DIM
    mean = _seg_sum(y, seg) * inv
    d = y - mean
    var = _seg_sum(d * d, seg) * inv
    return (d * lax.rsqrt(var + RW_GN_EPS) * gn_w + gn_b + bonus) * _silu(gate)


RW_PROMPT_PARAMS = ("mu_r", "mu_k", "mu_v", "mu_x", "w0", "w2", "a0", "a2", "k_k", "k_a", "r_k",
                    "gn_w", "gn_b", "seg")


def _rw_prompt_body(zr_ref, zk_ref, zv_ref, zx_ref, g_ref, *rest):
    npar = len(RW_PROMPT_PARAMS)
    prm = {name: ref[...] for name, ref in zip(RW_PROMPT_PARAMS, rest[:npar])}
    o_ref, sout_ref, s_ref = rest[npar:npar + 3]
    last = rest[npar + 3:]
    gi = pl.program_id(2)

    @pl.when(gi == 0)
    def _():
        s_ref[...] = jnp.zeros_like(s_ref)
        for l in last:
            l[...] = jnp.zeros_like(l)

    zs = [zr_ref[...], zk_ref[...], zv_ref[...], zx_ref[...]]
    prevs = []
    for z, l in zip(zs, last):
        first = _iota(z.shape, 0) == 0
        prevs.append(jnp.where(first, jnp.broadcast_to(l[0:1, :], z.shape), pltpu.roll(z, 1, 0)))
        l[...] = jnp.broadcast_to(z[z.shape[0] - 1:, :], l.shape)
    r, lw, k2, v, aa, bb, bonus = _rw_prepare(*zs, *prevs, prm)
    y = _rw_chunked_scan(r, lw, k2, v, aa, bb, s_ref)
    sout_ref[0] = s_ref[...]
    o_ref[...] = _rw_finish(y, bonus, g_ref[...], prm["gn_w"], prm["gn_b"], prm["seg"]).astype(o_ref.dtype)


def _rw_prompt(proj, rw, *, batch, seq, tb):
    n_rows = batch * seq
    ng = seq // tb
    n = RW_HEAD_DIM
    hp = LANES // n
    w = RW_WIDTH
    npair = w // LANES
    colblk = lambda off, per_pair: pl.BlockSpec(
        (tb, LANES), lambda b, p, g: (b * ng + g, off // LANES + (p if per_pair else 0)))
    vec = lambda c0: pl.BlockSpec((1, LANES), lambda b, p, g: (0, c0 + p))
    mat = pl.BlockSpec((LANES, LANES), lambda b, p, g: (0, p))
    fixed = lambda shape: pl.BlockSpec(shape, lambda b, p, g: (0, 0))
    param_specs = [vec(0), vec(npair), vec(2 * npair), pl.BlockSpec((1, LANES), lambda b, p, g: (0, 3 * npair)),
                   vec(0), mat, vec(0), mat, vec(0), vec(0), vec(0), vec(0), vec(0), fixed((LANES, LANES))]
    params = [rw["mu"], rw["mu"], rw["mu"], rw["mu"], rw["w0"], rw["w2"], rw["a0"], rw["a2"],
              rw["k_k"], rw["k_a"], rw["r_k"], rw["gn_w"], rw["gn_b"], rw["seg"]]
    out_spec = pl.BlockSpec((tb, LANES), lambda b, p, g: (b * ng + g, p))
    return pl.pallas_call(
        _rw_prompt_body,
        grid=(batch, npair, ng),
        in_specs=[colblk(OFF_RW, True), colblk(OFF_RW + w, True), colblk(OFF_RW + 2 * w, True),
                  colblk(OFF_RW_XWA, False), colblk(OFF_RW_G, True)] + param_specs,
        out_specs=[out_spec, pl.BlockSpec((1, hp, n, n), lambda b, p, g: (b, p, 0, 0))],
        out_shape=[jax.ShapeDtypeStruct((n_rows, w), BF16),
                   jax.ShapeDtypeStruct((batch, RW_HEADS, n, n), F32)],
        scratch_shapes=[pltpu.VMEM((hp, n, n), F32)] + [pltpu.VMEM((SUBLANES, LANES), F32)] * 4,
        compiler_params=_cparams(("parallel", "parallel", "arbitrary")),
        name="rw_prompt",
    )(*([proj] * 5), *params)


RW_SAMPLE_ROWS_PER_ITER = 2


def _rw_scan_sample_body(s0_ref, r_ref, lw_ref, k_ref, v_ref, aa_ref, bb_ref,
                         y_ref, sout_ref, st_ref, xt_ref, yt_ref, *, nb, t_new):
    n = RW_HEAD_DIM
    hv = LANES
    st_ref[...] = s0_ref[...].T.reshape(hv, n, nb)
    srcs = (r_ref, lw_ref, k_ref, aa_ref, bb_ref, v_ref)
    for t in range(t_new):
        for vi, src in enumerate(srcs):
            x = src[pl.ds(t, nb, stride=t_new), :]
            if vi == 1:
                x = jnp.exp(x)
            xt_ref[vi, t] = x.T

    rows_per_iter = RW_SAMPLE_ROWS_PER_ITER

    def row_group(ig, carry):
        i0 = ig * rows_per_iter
        k0 = pl.multiple_of((i0 // n) * n, n)
        rows = [i0 + j for j in range(rows_per_iter)]
        s = [st_ref[i] for i in rows]
        for t in range(t_new):
            rr = xt_ref[0, t, pl.ds(k0, n), :]
            ww = xt_ref[1, t, pl.ds(k0, n), :]
            kk = xt_ref[2, t, pl.ds(k0, n), :]
            aa = xt_ref[3, t, pl.ds(k0, n), :]
            bb = xt_ref[4, t, pl.ds(k0, n), :]
            vv = [xt_ref[5, t, pl.ds(i, 1), :] for i in rows]
            sa = _each(lambda x: jnp.sum(x * aa, axis=0, keepdims=True), s)
            s = _each(lambda x, a, v: x * ww + a * bb + v * kk, s, sa, vv)
            ys = _each(lambda x: jnp.sum(x * rr, axis=0, keepdims=True), s)
            for i, y in zip(rows, ys):
                yt_ref[t, pl.ds(i, 1), :] = y
        for i, x in zip(rows, s):
            st_ref[i] = x
        return carry

    lax.fori_loop(0, hv // rows_per_iter, row_group, 0)
    for t in range(t_new):
        y_ref[pl.ds(t, nb, stride=t_new), :] = yt_ref[t].T
    sout_ref[...] = st_ref[...].reshape(hv * n, nb).T


def _rw_scan_sample(pre, s0, *, nb, t_new):
    n = RW_HEAD_DIM
    tb = nb * t_new
    npair = RW_WIDTH // LANES
    sw = LANES * n
    s0f = s0.reshape(nb, RW_HEADS * n * n)
    spec = pl.BlockSpec((tb, LANES), lambda p: (0, p))
    sspec = pl.BlockSpec((nb, sw), lambda p: (0, p))
    y_new, s_new = pl.pallas_call(
        functools.partial(_rw_scan_sample_body, nb=nb, t_new=t_new),
        grid=(npair,),
        in_specs=[sspec] + [spec] * 6,
        out_specs=[spec, sspec],
        out_shape=[jax.ShapeDtypeStruct((tb, RW_WIDTH), F32), jax.ShapeDtypeStruct(s0f.shape, F32)],
        scratch_shapes=[pltpu.VMEM((LANES, n, nb), F32), pltpu.VMEM((6, t_new, LANES, nb), F32),
                        pltpu.VMEM((t_new, LANES, nb), F32)],
        compiler_params=_cparams(("arbitrary",)),
        name="rw_scan_sample",
    )(s0f, *pre)
    return y_new, s_new.reshape(s0.shape)


def _rw_post_body(y_ref, bonus_ref, g0, g1, g2, g3, gw_ref, gb_ref, seg_ref, o_ref):
    gate = jnp.concatenate([g0[...], g1[...], g2[...], g3[...]], axis=1)
    out = _rw_finish(y_ref[...], bonus_ref[...], gate, gw_ref[...], gb_ref[...], seg_ref[...])
    o_ref[...] = out.astype(o_ref.dtype)


def _rw_post(y, bonus, proj, rw, *, tb):
    n_rows = y.shape[0]
    w = RW_WIDTH
    row = pl.BlockSpec((tb, w), lambda i: (i, 0))
    gspecs = [pl.BlockSpec((tb, LANES), functools.partial(lambda i, q: (i, OFF_RW_G // LANES + q), q=q))
              for q in range(w // LANES)]
    const = lambda shape: pl.BlockSpec(shape, lambda i: (0, 0))
    return pl.pallas_call(
        _rw_post_body,
        grid=(n_rows // tb,),
        in_specs=[row, row] + gspecs + [const((1, w)), const((1, w)), const((w, w))],
        out_specs=row,
        out_shape=jax.ShapeDtypeStruct((n_rows, w), BF16),
        compiler_params=_cparams(("parallel",)),
        name="rw_post",
    )(y, bonus, proj, proj, proj, proj, rw["gn_w"], rw["gn_b"], rw["seg"])


def _block_diag(w):
    nb, d, _ = w.shape
    eye = jnp.eye(nb, dtype=w.dtype)
    return (eye[:, None, :, None] * w[:, :, None, :]).reshape(nb * d, nb * d)


def _layer_tail(x, proj, osb, olru, orw, p, lp, fg, final, nseq):
    h = _outproj(x, osb, olru, orw, p, lp["w_out"], lp["ple_gate"], lp["ple_proj"], fg,
                 layer=lp["layer"], tm=lp["tm_out"], final=final)
    t = proj.shape[0] // nseq
    p3 = proj.reshape(nseq, t, D_IN_PROJ)
    hist = CONV_WIDTH - 1
    return h, (proj, proj, p3[:, t - hist:, OFF_LRU_X:OFF_LRU_G], p3[:, t - 1, OFF_RW:OFF_RW_G])


def _kv_new_body(*refs, depth):
    ins, (ko_ref, vo_ref) = refs[:2 * depth], refs[2 * depth:]
    rows = ins[0].shape[0]
    for l in range(depth):
        for src, dst in ((ins[2 * l], ko_ref), (ins[2 * l + 1], vo_ref)):
            x = src[...]
            for h in range(SB_HEADS):
                dst[l, pl.ds(h, rows, stride=SB_HEADS), :] = x[:, h * SB_HEAD_DIM:(h + 1) * SB_HEAD_DIM]


def _kv_new(projs, *, tb):
    depth = len(projs)
    rows = projs[0].shape[0]
    col = lambda off: pl.BlockSpec((tb, SB_WIDTH), lambda i: (i, off // SB_WIDTH))
    out_spec = pl.BlockSpec((depth, tb * SB_HEADS, SB_HEAD_DIM), lambda i: (0, i, 0))
    out_shape = jax.ShapeDtypeStruct((depth, rows * SB_HEADS, SB_HEAD_DIM), F32)
    return pl.pallas_call(
        functools.partial(_kv_new_body, depth=depth),
        grid=(rows // tb,),
        in_specs=[col(OFF_SB_K), col(OFF_SB_V)] * depth,
        out_specs=[out_spec, out_spec],
        out_shape=[out_shape, out_shape],
        compiler_params=_cparams(("parallel",)),
        name="kv_new",
    )(*[p for proj in projs for p in (proj, proj)])


def _layer_prompt(x, p, lp, fg, *, batch, seq, final):
    proj = _inproj(x, lp["norm_g"], lp["w_in"], layer=lp["layer"], tm=lp["tm_in"], tn=lp["tn_in"])
    osb = _attn_prompt(proj, batch=batch, seq=seq, tq=lp["tq"])
    olru, h_last = _lru_prompt(proj, lp["lru"], batch=batch, seq=seq, tb=lp["lru_tb"])
    orw, s_new = _rw_prompt(proj, lp["rw"], batch=batch, seq=seq, tb=lp["rw_tb"])
    h, (k_new, v_new, conv_new, shift_new) = _layer_tail(x, proj, osb, olru, orw, p, lp, fg, final, batch)
    return h, (k_new, v_new, h_last.reshape(batch, LRU_WIDTH), conv_new, s_new, shift_new)


def _layer_sample(x, p, cache_k, cache_v, page_table, lru_conv, lru_h, rw_shift, rw_s, lp, fg, *,
                  layer, dec_batch, t_new, final):
    proj = _inproj(x, lp["norm_g"], lp["w_in"], layer=lp["layer"], tm=lp["tm_in"], tn=lp["tn_in"])
    osb = _attn_sample(proj, cache_k, cache_v, page_table, layer=layer, dec_batch=dec_batch, t_new=t_new,
                       bb=lp["attn_bb"])
    olru, h_last = _lru_sample(proj, lru_conv, lru_h, lp["lru"], nb=dec_batch, t_new=t_new)
    pre = _rw_pre_sample(proj, rw_shift, lp["rw"], nb=dec_batch, t_new=t_new)
    y_rw, s_new = _rw_scan_sample(pre[:6], rw_s, nb=dec_batch, t_new=t_new)
    orw = _rw_post(y_rw, pre[6], proj, lp["rw"], tb=lp["tm_in"])
    h, (k_new, v_new, conv_new, shift_new) = _layer_tail(x, proj, osb, olru, orw, p, lp, fg, final,
                                                          dec_batch)
    return h, (k_new, v_new, h_last, conv_new, s_new, shift_new)


def _layer_params(l, norm_g, w_in, w_out, lru_conv_w, lru_conv_b, lru_gate_a_w, lru_gate_a_b,
                  lru_gate_x_w, lru_gate_x_b, lru_lambda, rw_mu, rw_w0, rw_w2, rw_a0, rw_a2,
                  rw_k_k, rw_k_a, rw_r_k, rw_gn_w, rw_gn_b, ple_proj, ple_gate):
    row = lambda a: a[l].reshape(1, -1)
    w = RW_WIDTH
    zeros = jnp.zeros((RW_LORA, w), F32)
    head_id = jnp.arange(w) // RW_HEAD_DIM
    lru = dict(conv_w=lru_conv_w[l], conv_b=row(lru_conv_b),
               gate_w=jnp.concatenate([_block_diag(lru_gate_a_w[l]), _block_diag(lru_gate_x_w[l])],
                                      axis=1).astype(BF16),
               gate_a_b=row(lru_gate_a_b), gate_x_b=row(lru_gate_x_b), lam=row(lru_lambda))
    rw = dict(mu=row(rw_mu), w0=row(rw_w0),
              w2=jnp.concatenate([rw_w2[l], zeros], axis=0).astype(BF16),
              a0=row(rw_a0),
              a2=jnp.concatenate([zeros, rw_a2[l]], axis=0).astype(BF16),
              k_k=row(rw_k_k), k_a=row(rw_k_a), r_k=row(rw_r_k),
              gn_w=row(rw_gn_w), gn_b=row(rw_gn_b),
              seg=(head_id[:, None] == head_id[None, :]).astype(BF16))
    return dict(layer=l, norm_g=row(norm_g), w_in=w_in, w_out=w_out, ple_proj=ple_proj, ple_gate=ple_gate,
                lru=lru, rw=rw)


def _tiles(seq):
    return dict(tm_in=512, tn_in=D_IN_PROJ // 3, tq=128, attn_bb=4, lru_tb=min(512, seq),
                rw_tb=min(512, seq), tm_out=256)


def kernel(x_prompt, x_sample, p_prompt, p_sample, cache_sb_k, cache_sb_v, page_table, state_lru_h, state_lru_conv, state_rw_S, state_rw_shift, norm_g, w_in, w_out, lru_conv_w, lru_conv_b, lru_gate_a_w, lru_gate_a_b, lru_gate_x_w, lru_gate_x_b, lru_lambda, rw_mu, rw_w0, rw_w2, rw_a0, rw_a2, rw_k_k, rw_k_a, rw_r_k, rw_gn_w, rw_gn_b, ple_proj, ple_gate, final_norm_g):
    batch, seq, d = x_prompt.shape
    dec_batch, t_new, _ = x_sample.shape
    depth = w_in.shape[0]
    n_p, n_s = batch * seq, dec_batch * t_new
    xp = x_prompt.reshape(n_p, d)
    xs = x_sample.reshape(n_s, d)
    fg = final_norm_g.reshape(1, d)
    w_in, w_out, ple_proj, ple_gate = (a.astype(BF16) for a in (w_in, w_out, ple_proj, ple_gate))
    pp = p_prompt.reshape(depth, n_p, -1)
    ps = p_sample.reshape(depth, n_s, -1)
    outs_p, outs_s = [], []
    for l in range(depth):
        lp = _layer_params(l, norm_g, w_in, w_out, lru_conv_w, lru_conv_b, lru_gate_a_w, lru_gate_a_b,
                           lru_gate_x_w, lru_gate_x_b, lru_lambda, rw_mu, rw_w0, rw_w2, rw_a0, rw_a2,
                           rw_k_k, rw_k_a, rw_r_k, rw_gn_w, rw_gn_b, ple_proj, ple_gate)
        lp.update(_tiles(seq))
        final = l == depth - 1
        xp, st_p = _layer_prompt(xp, pp, lp, fg, batch=batch, seq=seq, final=final)
        xs, st_s = _layer_sample(xs, ps, cache_sb_k, cache_sb_v, page_table,
                                 state_lru_conv[l], state_lru_h[l], state_rw_shift[l], state_rw_S[l], lp, fg,
                                 layer=l, dec_batch=dec_batch, t_new=t_new, final=final)
        outs_p.append(st_p)
        outs_s.append(st_s)
    y_prompt = xp.reshape(batch, seq, d)
    y_sample = xs.reshape(dec_batch, t_new, d)
    stack = lambda outs, k: jnp.stack([o[k] for o in outs])
    tiles = _tiles(seq)
    k_p, v_p = _kv_new([o[0] for o in outs_p], tb=tiles["tm_in"])
    k_s, v_s = _kv_new([o[0] for o in outs_s], tb=tiles["tm_in"])
    heads_p = lambda a: a.reshape(depth, batch, seq, SB_HEADS, SB_HEAD_DIM)
    heads_s = lambda a: a.reshape(depth, dec_batch, t_new, SB_HEADS, SB_HEAD_DIM)
    return (y_prompt, y_sample, heads_p(k_p), heads_p(v_p), heads_s(k_s), heads_s(v_s),
            stack(outs_p, 2), stack(outs_s, 2), stack(outs_p, 3), stack(outs_s, 3),
            stack(outs_p, 4), stack(outs_s, 4), stack(outs_p, 5), stack(outs_s, 5))
```

```python
import functools
import math

import jax
import jax.numpy as jnp
from jax import lax
from jax.experimental import pallas as pl
from jax.experimental.pallas import tpu as pltpu

F32 = jnp.float32
BF16 = jnp.bfloat16

D_MODEL = 2048
D_PLE = 256
RMS_EPS = 1e-6
SB_HEADS = 8
SB_HEAD_DIM = 128
SB_WIDTH = SB_HEADS * SB_HEAD_DIM
LRU_WIDTH = 512
LRU_BLOCKS = 8
CONV_WIDTH = 4
LRU_C = 8.0
RW_WIDTH = 512
RW_HEAD_DIM = 64
RW_HEADS = 8
RW_LORA = 64
RW_GN_EPS = 64e-5
RW_SHIFT_WIDTH = 3 * RW_WIDTH + 2 * RW_LORA
OFF_SB_Q = 0
OFF_SB_K = OFF_SB_Q + SB_WIDTH
OFF_SB_V = OFF_SB_K + SB_WIDTH
OFF_SB_G = OFF_SB_V + SB_WIDTH
OFF_LRU_X = OFF_SB_G + SB_WIDTH
OFF_LRU_G = OFF_LRU_X + LRU_WIDTH
OFF_RW = OFF_LRU_G + LRU_WIDTH
OFF_RW_XWA = OFF_RW + 3 * RW_WIDTH
OFF_RW_G = OFF_RW + RW_SHIFT_WIDTH
D_IN_PROJ = OFF_RW_G + RW_WIDTH

LANES = 128
SUBLANES = 8
VMEM_LIMIT = 56 * 1024 * 1024

LOG_ZERO = -110.0
LOG_OFF = -1e30


def _cparams(sem, vmem=VMEM_LIMIT):
    return pltpu.CompilerParams(dimension_semantics=sem, vmem_limit_bytes=vmem)


def _dot(a, b):
    return jnp.dot(a, b, preferred_element_type=F32)


def _dot_nt(a, b):
    return lax.dot_general(a, b, (((1,), (1,)), ((), ())), preferred_element_type=F32)


def _dot_tn(a, b):
    return lax.dot_general(a, b, (((0,), (0,)), ((), ())), preferred_element_type=F32)


def _dot_hilo_lhs(x, m):
    rows = x.shape[0]
    hi = x.astype(BF16)
    lo = (x - hi.astype(F32)).astype(BF16)
    both = _dot(jnp.concatenate([hi, lo], axis=0), m)
    return both[:rows] + both[rows:]


def _dot_hilo_rhs(m, x):
    cols = x.shape[1]
    hi = x.astype(BF16)
    lo = (x - hi.astype(F32)).astype(BF16)
    both = _dot(m, jnp.concatenate([hi, lo], axis=1))
    return both[:, :cols] + both[:, cols:]


def _each(f, *lists):
    return [f(*xs) for xs in zip(*lists)]


def _iota(shape, dim):
    return lax.broadcasted_iota(jnp.int32, shape, dim)


def _silu(x):
    return x * jax.nn.sigmoid(x)


def _softplus(x):
    return jnp.maximum(x, 0.0) + jnp.log1p(jnp.exp(-jnp.abs(x)))


EXPM1_SERIES_TERMS = 7
EXPM1_SERIES_RANGE = 0.125


def _neg_expm1(x, ex):
    p = 1.0 + x * (1.0 / EXPM1_SERIES_TERMS)
    for n in range(EXPM1_SERIES_TERMS - 1, 1, -1):
        p = 1.0 + (x * (1.0 / n)) * p
    return jnp.where(x > -EXPM1_SERIES_RANGE, -x * p, 1.0 - ex)


def _log_sigmoid_pair(z):
    t = jnp.log(1.0 + jnp.exp(-jnp.abs(z)))
    return jnp.minimum(z, 0.0) - t, -jnp.maximum(z, 0.0) - t


def _inproj_body(x_ref, g_ref, w_ref, o_ref, u_ref):
    @pl.when(pl.program_id(1) == 0)
    def _():
        x = x_ref[...]
        ms = jnp.mean(x * x, axis=-1, keepdims=True)
        u_ref[...] = (x * lax.rsqrt(ms + RMS_EPS) * g_ref[...]).astype(BF16)

    o_ref[...] = _dot(u_ref[...], w_ref[...])


def _inproj(x, g, w, *, layer, tm, tn):
    n, d = x.shape
    dout = w.shape[2]
    return pl.pallas_call(
        _inproj_body,
        grid=(n // tm, dout // tn),
        in_specs=[pl.BlockSpec((tm, d), lambda i, j: (i, 0)),
                  pl.BlockSpec((1, d), lambda i, j: (0, 0)),
                  pl.BlockSpec((None, d, tn), lambda i, j: (layer, 0, j))],
        out_specs=pl.BlockSpec((tm, tn), lambda i, j: (i, j)),
        out_shape=jax.ShapeDtypeStruct((n, dout), F32),
        scratch_shapes=[pltpu.VMEM((tm, d), BF16)],
        compiler_params=_cparams(("parallel", "arbitrary")),
        name="inproj",
    )(x, g, w)


def _outproj_body(x_ref, osb_ref, olru_ref, orw_ref, p_ref, w1_ref, w2_ref, w3_ref,
                  wg_ref, wp_ref, fg_ref, o_ref, *, final):
    h = x_ref[...] + _dot(osb_ref[...], w1_ref[...]) + _dot(olru_ref[...], w2_ref[...]) \
        + _dot(orw_ref[...], w3_ref[...])
    gate = _dot(h.astype(BF16), wg_ref[...])
    pp = _dot(p_ref[...].astype(BF16), wp_ref[...])
    h = h + jax.nn.sigmoid(gate) * pp
    if final:
        ms = jnp.mean(h * h, axis=-1, keepdims=True)
        h = h * lax.rsqrt(ms + RMS_EPS) * fg_ref[...]
    o_ref[...] = h


def _outproj(x, osb, olru, orw, p, w_out, w_gate, w_ple, fg, *, layer, tm, final):
    n, d = x.shape
    wsb, wlru, wrw = osb.shape[1], olru.shape[1], orw.shape[1]
    dp = p.shape[2]
    row = lambda w: pl.BlockSpec((tm, w), lambda i: (i, 0))
    const = lambda shape, idx: pl.BlockSpec((None,) + shape, lambda i: (layer,) + idx)
    return pl.pallas_call(
        functools.partial(_outproj_body, final=final),
        grid=(n // tm,),
        in_specs=[row(d), row(wsb), row(wlru), row(wrw),
                  pl.BlockSpec((None, tm, dp), lambda i: (layer, i, 0)),
                  const((wsb, d), (0, 0)),
                  const((wlru, d), (wsb // wlru, 0)),
                  const((wrw, d), ((wsb + wlru) // wrw, 0)),
                  const((d, d), (0, 0)),
                  const((dp, d), (0, 0)),
                  pl.BlockSpec((1, d), lambda i: (0, 0))],
        out_specs=row(d),
        out_shape=jax.ShapeDtypeStruct((n, d), F32),
        compiler_params=_cparams(("parallel",)),
        name="outproj",
    )(x, osb, olru, orw, p, w_out, w_out, w_out, w_gate, w_ple, fg)


ATTN_NEAR_BLOCKS = 3


def _sb_blocks(qh, kfs, vfs, r_old, masks, gates, tri, scale):
    hd = SB_HEAD_DIM
    heads = range(len(qh))
    units = [(d, h) for d in range(len(kfs)) for h in heads]
    per_unit = lambda f: [f(d, h) for d, h in units]
    kh = per_unit(lambda d, h: kfs[d][:, h * hd:(h + 1) * hd].astype(BF16))
    vh = per_unit(lambda d, h: vfs[d][:, h * hd:(h + 1) * hd].astype(BF16))
    mask = per_unit(lambda d, h: masks[d])
    z = _each(lambda u, k: _dot_nt(qh[u[1]], k) * scale, units, kh)
    ls = _each(_log_sigmoid_pair, z)
    lsz = [x[0] for x in ls]
    lstay = _each(lambda x, m: x[1] if m is None else jnp.where(m, x[1], 0.0), ls, mask)
    s_excl = _each(lambda x: _dot_hilo_lhs(x, tri), lstay)
    total = _each(lambda s, x: s[:, 0:1] + x[:, 0:1], s_excl, lstay)
    r_in, r_run = [], list(r_old)
    for (d, h), t in zip(units, total):
        r_in.append(r_run[h] if gates[d] is None else r_run[h] + gates[d])
        r_run[h] = r_run[h] + t
    w = _each(lambda a, s, r: jnp.exp(a + s + r), lsz, s_excl, r_in)
    w = _each(lambda x, m: x if m is None else jnp.where(m, x, 0.0), w, mask)
    pv = _each(lambda x, v: _dot(x.astype(BF16), v), w, vh)
    acc = [functools.reduce(lambda a, b: a + b, [p for (d, hh), p in zip(units, pv) if hh == h])
           for h in heads]
    return acc, r_run


def _all_done(r_list):
    r_max = functools.reduce(jnp.maximum, r_list)
    return (jnp.max(r_max) < LOG_ZERO).astype(jnp.int32)


def _attn_prompt_body(*refs, tq, nq, scale):
    near = ATTN_NEAR_BLOCKS
    q_ref, g_ref = refs[0], refs[1]
    k_refs, v_refs = refs[2:2 + near], refs[2 + near:2 + 2 * near]
    proj_ref, o_ref, kbuf, vbuf, sem, acc_ref, r_ref = refs[2 + 2 * near:]
    b = pl.program_id(0)
    i = pl.program_id(1)
    hd = SB_HEAD_DIM
    heads = range(SB_HEADS)
    q = q_ref[...].astype(BF16)
    qh = [q[:, h * hd:(h + 1) * hd] for h in heads]
    row = _iota((tq, tq), 0)
    col = _iota((tq, tq), 1)
    tri = (row > col).astype(BF16)
    masks = [col < row] + [None] * (near - 1)
    gates = [None] + [jnp.where(i >= d, 0.0, LOG_OFF) for d in range(1, near)]
    acc, r = _sb_blocks(qh, [k[...] for k in k_refs], [v[...] for v in v_refs],
                        [jnp.zeros((tq, 1), F32) for _ in heads], masks, gates, tri, scale)
    for h in heads:
        acc_ref[h] = acc[h]
        r_ref[h] = r[h]

    def copies(j):
        rows = pl.ds(pl.multiple_of((b * nq + j) * tq, tq), tq)
        return (pltpu.make_async_copy(proj_ref.at[rows, pl.ds(OFF_SB_K, SB_WIDTH)], kbuf, sem.at[0]),
                pltpu.make_async_copy(proj_ref.at[rows, pl.ds(OFF_SB_V, SB_WIDTH)], vbuf, sem.at[1]))

    def cond(c):
        j, done = c
        return jnp.logical_and(j >= 0, done == 0)

    def body(c):
        j, _ = c
        for cp in copies(j):
            cp.start()
        for cp in copies(j):
            cp.wait()
        pv, r_new = _sb_blocks(qh, [kbuf[...]], [vbuf[...]], [r_ref[h] for h in heads], [None], [None],
                               tri, scale)
        for h in heads:
            acc_ref[h] += pv[h]
            r_ref[h] = r_new[h]
        return j - 1, _all_done(r_new)

    lax.while_loop(cond, body, (i - near, _all_done(r)))
    out = jnp.concatenate([acc_ref[h] for h in heads], axis=1)
    o_ref[...] = (out * _silu(g_ref[...])).astype(o_ref.dtype)


def _attn_prompt(proj, *, batch, seq, tq):
    n_rows = batch * seq
    nq = seq // tq
    w = SB_WIDTH
    scale = 1.0 / math.sqrt(SB_HEAD_DIM)
    near = ATTN_NEAR_BLOCKS
    cur = lambda off: pl.BlockSpec((tq, w), lambda b, i: (b * nq + i, off // w))
    back = lambda off, d: pl.BlockSpec((tq, w), lambda b, i: (b * nq + jnp.maximum(i - d, 0), off // w))
    return pl.pallas_call(
        functools.partial(_attn_prompt_body, tq=tq, nq=nq, scale=scale),
        grid=(batch, nq),
        in_specs=[cur(OFF_SB_Q), cur(OFF_SB_G)]
        + [back(OFF_SB_K, d) for d in range(near)] + [back(OFF_SB_V, d) for d in range(near)]
        + [pl.BlockSpec(memory_space=pl.ANY)],
        out_specs=pl.BlockSpec((tq, w), lambda b, i: (b * nq + i, 0)),
        out_shape=jax.ShapeDtypeStruct((n_rows, w), BF16),
        scratch_shapes=[pltpu.VMEM((tq, w), F32), pltpu.VMEM((tq, w), F32), pltpu.SemaphoreType.DMA((2,)),
                        pltpu.VMEM((SB_HEADS, tq, SB_HEAD_DIM), F32), pltpu.VMEM((SB_HEADS, tq, 1), F32)],
        compiler_params=_cparams(("parallel", "arbitrary")),
        name="attn_prompt",
    )(*([proj] * (2 + 2 * near + 1)))


ATTN_NEAR_PAGES = 2


def _sb_block_seqs(qh, kh, vh, r_old, mask, tri, scale):
    seqs = range(len(kh))
    heads = range(len(kh[0]))
    t = qh[0][0].shape[0]
    z = [[_dot_nt(qh[s][h], kh[s][h].astype(BF16)) for h in heads] for s in seqs]
    ls = [_log_sigmoid_pair(jnp.concatenate(z[s], axis=0) * scale) for s in seqs]
    lsz = [x[0] for x in ls]
    lstay = [x[1] for x in ls]
    if mask is not None:
        lstay = _each(lambda x: jnp.where(mask, x, 0.0), lstay)
    s_excl = _each(lambda x: _dot_hilo_lhs(x, tri), lstay)
    w = _each(lambda a, sx, r: jnp.exp(a + sx + r), lsz, s_excl, r_old)
    if mask is not None:
        w = _each(lambda x: jnp.where(mask, x, 0.0), w)
    pv = [[_dot(w[s][h * t:(h + 1) * t, :].astype(BF16), vh[s][h].astype(BF16)) for h in heads]
          for s in seqs]
    r_new = _each(lambda r, sx, x: r + sx[:, 0:1] + x[:, 0:1], r_old, s_excl, lstay)
    return pv, r_new


def _attn_sample_body(pt_ref, q_ref, kn_ref, vn_ref, g_ref, kc_ref, vc_ref, o_ref,
                      kbuf, vbuf, sem, acc_ref, r_ref, of_ref, *, layer, bb, t_new, n_pages, page, scale):
    i = pl.program_id(0)
    nsteps = pl.num_programs(0)
    heads = range(SB_HEADS)
    seqs = range(bb)
    hq = SB_HEADS * t_new
    hd = SB_HEAD_DIM
    near = min(ATTN_NEAR_PAGES, n_pages)
    extra_slot = 2 * bb * near
    par = i % 2

    def page_copies(b, j, slot):
        pg = pt_ref[b, j]
        return (pltpu.make_async_copy(kc_ref.at[layer, pg], kbuf.at[slot], sem.at[0, slot]),
                pltpu.make_async_copy(vc_ref.at[layer, pg], vbuf.at[slot], sem.at[1, slot]))

    def near_slot(parity, s, pi):
        return (parity * bb + s) * near + pi

    def near_copies(step, parity):
        return [c for s in seqs for pi in range(near)
                for c in page_copies(step * bb + s, n_pages - near + pi, near_slot(parity, s, pi))]

    @pl.when(i == 0)
    def _():
        for c in near_copies(i, par):
            c.start()

    @pl.when(i + 1 < nsteps)
    def _():
        for c in near_copies(i + 1, 1 - par):
            c.start()

    tri_new = (_iota((t_new, t_new), 0) > _iota((t_new, t_new), 1)).astype(BF16)
    npg = near * page
    tri_near = (_iota((npg, npg), 0) > _iota((npg, npg), 1)).astype(BF16)
    tri_page = tri_near[:page, :page]
    new_mask = _iota((hq, t_new), 1) < (_iota((hq, t_new), 0) % t_new)
    split = lambda x: [x[:, h * hd:(h + 1) * hd] for h in heads]
    head_rows = lambda buf, slot, h: buf[slot, pl.ds(h, page, stride=SB_HEADS), :]

    qh = [split(q_ref[s * t_new:(s + 1) * t_new, :].astype(BF16)) for s in seqs]
    r = [jnp.zeros((hq, 1), F32) for _ in seqs]
    kh = [split(kn_ref[s * t_new:(s + 1) * t_new, :]) for s in seqs]
    vh = [split(vn_ref[s * t_new:(s + 1) * t_new, :]) for s in seqs]
    acc, r = _sb_block_seqs(qh, kh, vh, r, new_mask, tri_new, scale)
    for c in near_copies(i, par):
        c.wait()
    page_rows = lambda buf, s, h: jnp.concatenate(
        [head_rows(buf, near_slot(par, s, pi), h) for pi in range(near)], axis=0)
    kh = [[page_rows(kbuf, s, h) for h in heads] for s in seqs]
    vh = [[page_rows(vbuf, s, h) for h in heads] for s in seqs]
    pv, r = _sb_block_seqs(qh, kh, vh, r, None, tri_near, scale)
    for s in seqs:
        for h in heads:
            acc_ref[s, h] = acc[s][h] + pv[s][h]
        r_ref[s] = r[s]

    for s in seqs:
        b = i * bb + s

        def cond(c):
            j, done = c
            return jnp.logical_and(j >= 0, done == 0)

        def body(c, b=b, s=s):
            j, _ = c
            for cp in page_copies(b, j, extra_slot):
                cp.start()
            for cp in page_copies(b, j, extra_slot):
                cp.wait()
            kj = [[head_rows(kbuf, extra_slot, h) for h in heads]]
            vj = [[head_rows(vbuf, extra_slot, h) for h in heads]]
            pv_j, r_j = _sb_block_seqs([qh[s]], kj, vj, [r_ref[s]], None, tri_page, scale)
            for h in heads:
                acc_ref[s, h] += pv_j[0][h]
            r_ref[s] = r_j[0]
            return j - 1, _all_done(r_j)

        lax.while_loop(cond, body, (jnp.int32(n_pages - near - 1), _all_done([r[s]])))
        o_b = jnp.concatenate([acc_ref[s, h] for h in heads], axis=1)
        rows = slice(s * t_new, (s + 1) * t_new)
        of_ref[rows, :] = o_b * _silu(g_ref[rows, :])
    o_ref[...] = of_ref[...].astype(o_ref.dtype)


def _attn_sample(proj, cache_k, cache_v, page_table, *, layer, dec_batch, t_new, bb):
    depth, n_phys, page = cache_k.shape[:3]
    n_pages = page_table.shape[1]
    tb = bb * t_new
    spec = lambda off: pl.BlockSpec((tb, SB_WIDTH), lambda i, pt: (i, off // SB_WIDTH))
    hq = SB_HEADS * t_new
    nslots = 2 * bb * min(ATTN_NEAR_PAGES, n_pages) + 1
    cache_k = cache_k.reshape(depth, n_phys, page * SB_HEADS, SB_HEAD_DIM)
    cache_v = cache_v.reshape(depth, n_phys, page * SB_HEADS, SB_HEAD_DIM)
    pagebuf = pltpu.VMEM((nslots, page * SB_HEADS, SB_HEAD_DIM), F32)
    grid_spec = pltpu.PrefetchScalarGridSpec(
        num_scalar_prefetch=1,
        grid=(dec_batch // bb,),
        in_specs=[spec(OFF_SB_Q), spec(OFF_SB_K), spec(OFF_SB_V), spec(OFF_SB_G),
                  pl.BlockSpec(memory_space=pl.ANY), pl.BlockSpec(memory_space=pl.ANY)],
        out_specs=pl.BlockSpec((tb, SB_WIDTH), lambda i, pt: (i, 0)),
        scratch_shapes=[pagebuf, pagebuf, pltpu.SemaphoreType.DMA((2, nslots)),
                        pltpu.VMEM((bb, SB_HEADS, t_new, SB_HEAD_DIM), F32), pltpu.VMEM((bb, hq, 1), F32),
                        pltpu.VMEM((tb, SB_WIDTH), F32)],
    )
    return pl.pallas_call(
        functools.partial(_attn_sample_body, layer=layer, bb=bb, t_new=t_new, n_pages=n_pages, page=page,
                          scale=1.0 / math.sqrt(SB_HEAD_DIM)),
        grid_spec=grid_spec,
        out_shape=jax.ShapeDtypeStruct((dec_batch * t_new, SB_WIDTH), BF16),
        compiler_params=_cparams(("arbitrary",)),
        name="attn_sample",
    )(page_table, proj, proj, proj, proj, cache_k, cache_v)


def _lru_gates(xc, gw_ref, ga_b_ref, gx_b_ref, lam_ref):
    w = xc.shape[-1]
    gates = _dot(xc.astype(BF16), gw_ref[...])
    r = jax.nn.sigmoid(gates[:, :w] + ga_b_ref[...])
    i = jax.nn.sigmoid(gates[:, w:] + gx_b_ref[...])
    log_a = -LRU_C * r * _softplus(-lam_ref[...])
    a = jnp.exp(log_a)
    mult = jnp.sqrt(_neg_expm1(2.0 * log_a, a * a))
    return a, mult * (i * xc)


def _lru_prompt_body(x_ref, g_ref, cw_ref, cb_ref, gw_ref, ga_b_ref, gx_b_ref, lam_ref,
                     o_ref, hl_ref, xp_ref, a_ref, b_ref, h_ref, *, tb):
    c = pl.program_id(1)
    pad = SUBLANES
    hist = CONV_WIDTH - 1

    @pl.when(c == 0)
    def _():
        xp_ref[0:pad, :] = jnp.zeros((pad, LRU_WIDTH), F32)
        h_ref[...] = jnp.zeros_like(h_ref)

    xb = x_ref[...]
    xp_ref[pad:pad + tb, :] = xb
    xc = cb_ref[...] + xp_ref[pad - hist:pad - hist + tb, :] * cw_ref[0:1, :]
    for j in range(1, CONV_WIDTH):
        xc = xc + xp_ref[pad - hist + j:pad - hist + j + tb, :] * cw_ref[j:j + 1, :]
    xp_ref[pad - hist:pad, :] = xb[tb - hist:tb, :]

    a, bt = _lru_gates(xc, gw_ref, ga_b_ref, gx_b_ref, lam_ref)
    rowg = _iota((tb, LRU_WIDTH), 0) % SUBLANES
    s = 1
    while s < SUBLANES:
        a_sh = pltpu.roll(a, s, 0)
        b_sh = pltpu.roll(bt, s, 0)
        ok = rowg >= s
        bt = jnp.where(ok, a * b_sh + bt, bt)
        a = jnp.where(ok, a * a_sh, a)
        s *= 2
    a_ref[...] = a
    b_ref[...] = bt

    def group(gi, h):
        r0 = pl.multiple_of(gi * SUBLANES, SUBLANES)
        hs = a_ref[pl.ds(r0, SUBLANES), :] * h + b_ref[pl.ds(r0, SUBLANES), :]
        b_ref[pl.ds(r0, SUBLANES), :] = hs
        return jnp.broadcast_to(hs[SUBLANES - 1:SUBLANES, :], (SUBLANES, LRU_WIDTH))

    h = lax.fori_loop(0, tb // SUBLANES, group, h_ref[...])
    h_ref[...] = h
    hl_ref[0] = h[0:1, :]
    o_ref[...] = (b_ref[...] * _silu(g_ref[...])).astype(o_ref.dtype)


def _lru_prompt(proj, lw, *, batch, seq, tb):
    n_rows = batch * seq
    nc = seq // tb
    w = LRU_WIDTH
    const = lambda shape: pl.BlockSpec(shape, lambda b, c: (0,) * len(shape))
    return pl.pallas_call(
        functools.partial(_lru_prompt_body, tb=tb),
        grid=(batch, nc),
        in_specs=[pl.BlockSpec((tb, w), lambda b, c: (b * nc + c, OFF_LRU_X // w)),
                  pl.BlockSpec((tb, w), lambda b, c: (b * nc + c, OFF_LRU_G // w)),
                  const((CONV_WIDTH, w)), const((1, w)), const((w, 2 * w)),
                  const((1, w)), const((1, w)), const((1, w))],
        out_specs=[pl.BlockSpec((tb, w), lambda b, c: (b * nc + c, 0)),
                   pl.BlockSpec((1, 1, w), lambda b, c: (b, 0, 0))],
        out_shape=[jax.ShapeDtypeStruct((n_rows, w), BF16),
                   jax.ShapeDtypeStruct((batch, 1, w), F32)],
        scratch_shapes=[pltpu.VMEM((SUBLANES + tb, w), F32), pltpu.VMEM((tb, w), F32),
                        pltpu.VMEM((tb, w), F32), pltpu.VMEM((SUBLANES, w), F32)],
        compiler_params=_cparams(("arbitrary", "arbitrary")),
        name="lru_prompt",
    )(proj, proj, lw["conv_w"], lw["conv_b"], lw["gate_w"], lw["gate_a_b"], lw["gate_x_b"], lw["lam"])


def _lru_sample_body(*refs, nb, t_new):
    ncb = LRU_WIDTH // LANES
    x_refs, cbuf_refs = refs[:ncb], refs[ncb:2 * ncb]
    (g_ref, h0_ref, cw_ref, cb_ref, gw_ref, ga_b_ref, gx_b_ref, lam_ref,
     o_ref, hl_ref, of_ref) = refs[2 * ncb:]
    hist = CONV_WIDTH - 1
    gather = lambda rs, start, stride: jnp.concatenate(
        [r[pl.ds(start, nb, stride=stride), :] for r in rs], axis=1)
    xs = [gather(cbuf_refs, j, hist) for j in range(hist)]
    xs += [gather(x_refs, t, t_new) for t in range(t_new)]
    h = h0_ref[...]
    for t in range(t_new):
        xc = cb_ref[...]
        for j in range(CONV_WIDTH):
            xc = xc + xs[t + j] * cw_ref[j:j + 1, :]
        a, bt = _lru_gates(xc, gw_ref, ga_b_ref, gx_b_ref, lam_ref)
        h = a * h + bt
        for cb in range(of_ref.shape[0]):
            of_ref[cb, pl.ds(t, nb, stride=t_new), :] = h[:, cb * LANES:(cb + 1) * LANES]
    hl_ref[...] = h
    hs = jnp.concatenate([of_ref[cb] for cb in range(of_ref.shape[0])], axis=1)
    o_ref[...] = (hs * _silu(g_ref[...])).astype(o_ref.dtype)


def _lru_sample(proj, conv_buf, h0, lw, *, nb, t_new):
    w = LRU_WIDTH
    tb = nb * t_new
    const = lambda shape: pl.BlockSpec(shape, lambda i: (0,) * len(shape))
    ncb = w // LANES
    hist = CONV_WIDTH - 1
    colblk = lambda rows, c: pl.BlockSpec((rows, LANES), lambda i: (0, c))
    return pl.pallas_call(
        functools.partial(_lru_sample_body, nb=nb, t_new=t_new),
        grid=(1,),
        in_specs=[colblk(tb, OFF_LRU_X // LANES + q) for q in range(ncb)]
        + [colblk(nb * hist, q) for q in range(ncb)]
        + [pl.BlockSpec((tb, w), lambda i: (0, OFF_LRU_G // w)), const((nb, w)),
           const((CONV_WIDTH, w)), const((1, w)), const((w, 2 * w)),
           const((1, w)), const((1, w)), const((1, w))],
        out_specs=[const((tb, w)), const((nb, w))],
        out_shape=[jax.ShapeDtypeStruct((tb, w), BF16),
                   jax.ShapeDtypeStruct((nb, w), F32)],
        scratch_shapes=[pltpu.VMEM((ncb, tb, LANES), F32)],
        compiler_params=_cparams(("arbitrary",)),
        name="lru_sample",
    )(*([proj] * ncb), *([conv_buf.reshape(nb * hist, w)] * ncb), proj, h0,
      lw["conv_w"], lw["conv_b"], lw["gate_w"], lw["gate_a_b"], lw["gate_x_b"], lw["lam"])


def _seg_sum(x, seg):
    return _dot_hilo_lhs(x, seg)


def _rw_prepare(z_r, z_k, z_v, z_x, p_r, p_k, p_v, p_x, rp):
    mix = lambda z, p, mu: z + mu * (p - z)
    r = mix(z_r, p_r, rp["mu_r"])
    k = mix(z_k, p_k, rp["mu_k"])
    v = mix(z_v, p_v, rp["mu_v"])
    x = mix(z_x, p_x, rp["mu_x"])
    seg = rp["seg"]
    w_lin = rp["w0"] + _dot(jnp.tanh(x).astype(BF16), rp["w2"])
    w_log = _log_sigmoid_pair(w_lin)[0] - 0.5
    logw = -jnp.exp(w_log)
    a = jax.nn.sigmoid(rp["a0"] + _dot(x.astype(BF16), rp["a2"]))
    kk = k * rp["k_k"]
    kk = kk * lax.rsqrt(_seg_sum(kk * kk, seg) + 1e-12)
    k2 = k * (1.0 + (a - 1.0) * rp["k_a"])
    bonus = _seg_sum(r * k2 * rp["r_k"], seg) * v
    return r, logw, k2, v, -kk, kk * a, bonus


def _rw_params(refs):
    names = ("mu", "w0", "w2", "a0", "a2", "k_k", "k_a", "r_k", "seg")
    prm = {n: r[...] for n, r in zip(names, refs)}
    w = RW_WIDTH
    mu = prm.pop("mu")
    prm.update(mu_r=mu[:, 0:w], mu_k=mu[:, w:2 * w], mu_v=mu[:, 2 * w:3 * w], mu_x=mu[:, 3 * w:])
    return prm


def _rw_param_specs(idx):
    w = RW_WIDTH
    shapes = [(1, RW_SHIFT_WIDTH), (1, w), (2 * RW_LORA, w), (1, w), (2 * RW_LORA, w),
              (1, w), (1, w), (1, w), (w, w)]
    return [pl.BlockSpec(s, idx) for s in shapes]


def _rw_param_args(rw):
    return [rw[n] for n in ("mu", "w0", "w2", "a0", "a2", "k_k", "k_a", "r_k", "seg")]


def _rw_pre_specs(row_idx):
    w = RW_WIDTH
    xw = 2 * RW_LORA
    return [(w, OFF_RW // w), (w, OFF_RW // w + 1), (w, OFF_RW // w + 2), (xw, OFF_RW_XWA // xw)]


def _rw_pre_sample_body(zr_ref, zk_ref, zv_ref, zx_ref, sr_ref, sk_ref, sv_ref, sx_ref, *rest,
                        nb, t_new):
    prm = _rw_params(rest[:9])
    outs = rest[9:16]
    spread_refs = rest[16:20]
    zs = [zr_ref[...], zk_ref[...], zv_ref[...], zx_ref[...]]
    prevs = []
    for z, s_ref, e_ref in zip(zs, (sr_ref, sk_ref, sv_ref, sx_ref), spread_refs):
        e_ref[...] = jnp.zeros_like(e_ref)
        for cb in range(e_ref.shape[0]):
            e_ref[cb, pl.ds(0, nb, stride=t_new), :] = s_ref[:, cb * LANES:(cb + 1) * LANES]
        spread = jnp.concatenate([e_ref[cb] for cb in range(e_ref.shape[0])], axis=1)
        first = (_iota(z.shape, 0) % t_new) == 0
        prevs.append(jnp.where(first, spread, pltpu.roll(z, 1, 0)))
    res = _rw_prepare(*zs, *prevs, prm)
    for o, val in zip(outs, res):
        o[...] = val


def _rw_pre_sample(proj, shift0, rw, *, nb, t_new):
    w = RW_WIDTH
    tb = nb * t_new
    cols = _rw_pre_specs(None)
    in_specs = [pl.BlockSpec((tb, cw), functools.partial(lambda i, cb: (0, cb), cb=cb)) for cw, cb in cols]
    shifts = [shift0[:, 0:w], shift0[:, w:2 * w], shift0[:, 2 * w:3 * w], shift0[:, 3 * w:]]
    in_specs += [pl.BlockSpec(s.shape, lambda i: (0, 0)) for s in shifts]
    in_specs += _rw_param_specs(lambda i: (0, 0))
    out_spec = pl.BlockSpec((tb, w), lambda i: (0, 0))
    return pl.pallas_call(
        functools.partial(_rw_pre_sample_body, nb=nb, t_new=t_new),
        grid=(1,),
        in_specs=in_specs,
        out_specs=[out_spec] * 7,
        out_shape=[jax.ShapeDtypeStruct((tb, w), F32)] * 7,
        scratch_shapes=[pltpu.VMEM((cw // LANES, tb, LANES), F32) for cw, _ in cols],
        compiler_params=_cparams(("arbitrary",)),
        name="rw_pre_sample",
    )(proj, proj, proj, proj, *shifts, *_rw_param_args(rw))


RW_CHUNK = 64


def _rw_chunk_prep(units, masks):
    strict, incl, eye = masks
    r, lw_incl, lw_excl, l_tot, k, v, aa, bb = (list(x) for x in zip(*units))
    c = r[0].shape[0]
    bf = lambda x: x.astype(BF16)
    zero = jnp.zeros((), F32)
    at = _each(lambda a, l: bf(a * jnp.exp(l)), aa, lw_excl)
    rt = _each(lambda a, l: a * jnp.exp(l), r, lw_incl)
    en = _each(lambda l: jnp.exp(-l), lw_incl)
    kh = _each(lambda a, e: bf(a * e), k, en)
    bh = _each(lambda a, e: bf(a * e), bb, en)
    er = _each(lambda lt, l: jnp.exp(lt - l), l_tot, lw_incl)
    kb = _each(lambda a, e: bf(a * e), k, er)
    bbar = _each(lambda a, e: bf(a * e), bb, er)
    vb = _each(bf, v)
    ar = _each(lambda a, b: jnp.concatenate([a, bf(b)], axis=0), at, rt)
    mk = _each(_dot_nt, ar, kh)
    mb = _each(_dot_nt, ar, bh)
    m_ak = _each(lambda m: bf(jnp.where(strict, m[:c], zero)), mk)
    m_ab = _each(lambda m: jnp.where(strict, m[:c], zero), mb)
    m_rk = _each(lambda m: bf(jnp.where(incl, m[c:], zero)), mk)
    m_rb = _each(lambda m: bf(jnp.where(incl, m[c:], zero)), mb)
    p = m_ab
    t = _each(lambda m: jnp.where(eye, 1.0, zero) + m, m_ab)
    steps = max(1, (c - 1).bit_length()) - 1
    for _ in range(steps):
        pb = _each(bf, p)
        p = _each(_dot, pb, pb)
        t = _each(lambda t_, p_: t_ + _dot(bf(t_), bf(p_)), t, p)
    tb_ = _each(bf, t)
    mv = _each(lambda m, x: bf(_dot(m, x)), m_ak, vb)
    at2 = _each(lambda a, b: bf(_dot(a, b)), tb_, at)
    vp = _each(lambda a, b: bf(_dot(a, b)), tb_, mv)
    rp = _each(lambda a, m, x: bf(a + _dot(m, x)), rt, m_rb, at2)
    y0 = _each(lambda m1, x1, m2, x2: _dot(m1, x1) + _dot(m2, x2), m_rk, vb, m_rb, vp)
    g = _each(lambda a, b: bf(_dot_tn(a, b)), bbar, at2)
    hp = _each(lambda a, b, c_, d: _dot_tn(a, b) + _dot_tn(c_, d), vp, bbar, vb, kb)
    decay = _each(jnp.exp, l_tot)
    return list(zip(rp, y0, g, hp, decay))


def _rw_chunk_apply(prep, s0):
    rp, y0, g, hp, decay = prep
    s0b = s0.astype(BF16)
    return _dot_nt(rp, s0b) + y0, s0 * decay + _dot_nt(s0b, g) + hp


def _rw_chunked_scan(r, lw, k, v, aa, bb, s_ref):
    c = RW_CHUNK
    n = RW_HEAD_DIM
    hp = LANES // n
    nchunk = r.shape[0] // c
    row = _iota((c, c), 0)
    col = _iota((c, c), 1)
    masks = (col < row, col <= row, col == row)
    tri_incl = (col <= row).astype(BF16)

    units = []
    for ci in range(nchunk):
        rows = slice(ci * c, (ci + 1) * c)
        lw_c = lw[rows, :]
        lw_incl = _dot_hilo_rhs(tri_incl, lw_c)
        lw_excl = lw_incl - lw_c
        l_tot = lw_incl[c - 1:c, :]
        for hh in range(hp):
            sl = slice(hh * n, (hh + 1) * n)
            units.append((r[rows, sl], lw_incl[:, sl], lw_excl[:, sl], l_tot[:, sl], k[rows, sl],
                          v[rows, sl], aa[rows, sl], bb[rows, sl]))
    preps = _rw_chunk_prep(units, masks)
    states = [s_ref[hh] for hh in range(hp)]
    ys = []
    for ci in range(nchunk):
        y_h = []
        for hh in range(hp):
            y, states[hh] = _rw_chunk_apply(preps[ci * hp + hh], states[hh])
            y_h.append(y)
        ys.append(jnp.concatenate(y_h, axis=1))
    for hh in range(hp):
        s_ref[hh] = states[hh]
    return jnp.concatenate(ys, axis=0)


def _rw_finish(y, bonus, gate, gn_w, gn_b, seg):
    inv = 1.0 / RW_HEAD_DIM
    mean = _seg_sum(y, seg) * inv
    d = y - mean
    var = _seg_sum(d * d, seg) * inv
    return (d * lax.rsqrt(var + RW_GN_EPS) * gn_w + gn_b + bonus) * _silu(gate)


RW_PROMPT_PARAMS = ("mu_r", "mu_k", "mu_v", "mu_x", "w0", "w2", "a0", "a2", "k_k", "k_a", "r_k",
                    "gn_w", "gn_b", "seg")


def _rw_prompt_body(zr_ref, zk_ref, zv_ref, zx_ref, g_ref, *rest):
    npar = len(RW_PROMPT_PARAMS)
    prm = {name: ref[...] for name, ref in zip(RW_PROMPT_PARAMS, rest[:npar])}
    o_ref, sout_ref, s_ref = rest[npar:npar + 3]
    last = rest[npar + 3:]
    gi = pl.program_id(2)

    @pl.when(gi == 0)
    def _():
        s_ref[...] = jnp.zeros_like(s_ref)
        for l in last:
            l[...] = jnp.zeros_like(l)

    zs = [zr_ref[...], zk_ref[...], zv_ref[...], zx_ref[...]]
    prevs = []
    for z, l in zip(zs, last):
        first = _iota(z.shape, 0) == 0
        prevs.append(jnp.where(first, jnp.broadcast_to(l[0:1, :], z.shape), pltpu.roll(z, 1, 0)))
        l[...] = jnp.broadcast_to(z[z.shape[0] - 1:, :], l.shape)
    r, lw, k2, v, aa, bb, bonus = _rw_prepare(*zs, *prevs, prm)
    y = _rw_chunked_scan(r, lw, k2, v, aa, bb, s_ref)
    sout_ref[0] = s_ref[...]
    o_ref[...] = _rw_finish(y, bonus, g_ref[...], prm["gn_w"], prm["gn_b"], prm["seg"]).astype(o_ref.dtype)


def _rw_prompt(proj, rw, *, batch, seq, tb):
    n_rows = batch * seq
    ng = seq // tb
    n = RW_HEAD_DIM
    hp = LANES // n
    w = RW_WIDTH
    npair = w // LANES
    colblk = lambda off, per_pair: pl.BlockSpec(
        (tb, LANES), lambda b, p, g: (b * ng + g, off // LANES + (p if per_pair else 0)))
    vec = lambda c0: pl.BlockSpec((1, LANES), lambda b, p, g: (0, c0 + p))
    mat = pl.BlockSpec((LANES, LANES), lambda b, p, g: (0, p))
    fixed = lambda shape: pl.BlockSpec(shape, lambda b, p, g: (0, 0))
    param_specs = [vec(0), vec(npair), vec(2 * npair), pl.BlockSpec((1, LANES), lambda b, p, g: (0, 3 * npair)),
                   vec(0), mat, vec(0), mat, vec(0), vec(0), vec(0), vec(0), vec(0), fixed((LANES, LANES))]
    params = [rw["mu"], rw["mu"], rw["mu"], rw["mu"], rw["w0"], rw["w2"], rw["a0"], rw["a2"],
              rw["k_k"], rw["k_a"], rw["r_k"], rw["gn_w"], rw["gn_b"], rw["seg"]]
    out_spec = pl.BlockSpec((tb, LANES), lambda b, p, g: (b * ng + g, p))
    return pl.pallas_call(
        _rw_prompt_body,
        grid=(batch, npair, ng),
        in_specs=[colblk(OFF_RW, True), colblk(OFF_RW + w, True), colblk(OFF_RW + 2 * w, True),
                  colblk(OFF_RW_XWA, False), colblk(OFF_RW_G, True)] + param_specs,
        out_specs=[out_spec, pl.BlockSpec((1, hp, n, n), lambda b, p, g: (b, p, 0, 0))],
        out_shape=[jax.ShapeDtypeStruct((n_rows, w), BF16),
                   jax.ShapeDtypeStruct((batch, RW_HEADS, n, n), F32)],
        scratch_shapes=[pltpu.VMEM((hp, n, n), F32)] + [pltpu.VMEM((SUBLANES, LANES), F32)] * 4,
        compiler_params=_cparams(("parallel", "parallel", "arbitrary")),
        name="rw_prompt",
    )(*([proj] * 5), *params)


RW_SAMPLE_ROWS_PER_ITER = 2


def _rw_scan_sample_body(s0_ref, r_ref, lw_ref, k_ref, v_ref, aa_ref, bb_ref,
                         y_ref, sout_ref, st_ref, xt_ref, yt_ref, *, nb, t_new):
    n = RW_HEAD_DIM
    hv = LANES
    st_ref[...] = s0_ref[...].T.reshape(hv, n, nb)
    srcs = (r_ref, lw_ref, k_ref, aa_ref, bb_ref, v_ref)
    for t in range(t_new):
        for vi, src in enumerate(srcs):
            x = src[pl.ds(t, nb, stride=t_new), :]
            if vi == 1:
                x = jnp.exp(x)
            xt_ref[vi, t] = x.T

    rows_per_iter = RW_SAMPLE_ROWS_PER_ITER

    def row_group(ig, carry):
        i0 = ig * rows_per_iter
        k0 = pl.multiple_of((i0 // n) * n, n)
        rows = [i0 + j for j in range(rows_per_iter)]
        s = [st_ref[i] for i in rows]
        for t in range(t_new):
            rr = xt_ref[0, t, pl.ds(k0, n), :]
            ww = xt_ref[1, t, pl.ds(k0, n), :]
            kk = xt_ref[2, t, pl.ds(k0, n), :]
            aa = xt_ref[3, t, pl.ds(k0, n), :]
            bb = xt_ref[4, t, pl.ds(k0, n), :]
            vv = [xt_ref[5, t, pl.ds(i, 1), :] for i in rows]
            sa = _each(lambda x: jnp.sum(x * aa, axis=0, keepdims=True), s)
            s = _each(lambda x, a, v: x * ww + a * bb + v * kk, s, sa, vv)
            ys = _each(lambda x: jnp.sum(x * rr, axis=0, keepdims=True), s)
            for i, y in zip(rows, ys):
                yt_ref[t, pl.ds(i, 1), :] = y
        for i, x in zip(rows, s):
            st_ref[i] = x
        return carry

    lax.fori_loop(0, hv // rows_per_iter, row_group, 0)
    for t in range(t_new):
        y_ref[pl.ds(t, nb, stride=t_new), :] = yt_ref[t].T
    sout_ref[...] = st_ref[...].reshape(hv * n, nb).T


def _rw_scan_sample(pre, s0, *, nb, t_new):
    n = RW_HEAD_DIM
    tb = nb * t_new
    npair = RW_WIDTH // LANES
    sw = LANES * n
    s0f = s0.reshape(nb, RW_HEADS * n * n)
    spec = pl.BlockSpec((tb, LANES), lambda p: (0, p))
    sspec = pl.BlockSpec((nb, sw), lambda p: (0, p))
    y_new, s_new = pl.pallas_call(
        functools.partial(_rw_scan_sample_body, nb=nb, t_new=t_new),
        grid=(npair,),
        in_specs=[sspec] + [spec] * 6,
        out_specs=[spec, sspec],
        out_shape=[jax.ShapeDtypeStruct((tb, RW_WIDTH), F32), jax.ShapeDtypeStruct(s0f.shape, F32)],
        scratch_shapes=[pltpu.VMEM((LANES, n, nb), F32), pltpu.VMEM((6, t_new, LANES, nb), F32),
                        pltpu.VMEM((t_new, LANES, nb), F32)],
        compiler_params=_cparams(("arbitrary",)),
        name="rw_scan_sample",
    )(s0f, *pre)
    return y_new, s_new.reshape(s0.shape)


def _rw_post_body(y_ref, bonus_ref, g0, g1, g2, g3, gw_ref, gb_ref, seg_ref, o_ref):
    gate = jnp.concatenate([g0[...], g1[...], g2[...], g3[...]], axis=1)
    out = _rw_finish(y_ref[...], bonus_ref[...], gate, gw_ref[...], gb_ref[...], seg_ref[...])
    o_ref[...] = out.astype(o_ref.dtype)


def _rw_post(y, bonus, proj, rw, *, tb):
    n_rows = y.shape[0]
    w = RW_WIDTH
    row = pl.BlockSpec((tb, w), lambda i: (i, 0))
    gspecs = [pl.BlockSpec((tb, LANES), functools.partial(lambda i, q: (i, OFF_RW_G // LANES + q), q=q))
              for q in range(w // LANES)]
    const = lambda shape: pl.BlockSpec(shape, lambda i: (0, 0))
    return pl.pallas_call(
        _rw_post_body,
        grid=(n_rows // tb,),
        in_specs=[row, row] + gspecs + [const((1, w)), const((1, w)), const((w, w))],
        out_specs=row,
        out_shape=jax.ShapeDtypeStruct((n_rows, w), BF16),
        compiler_params=_cparams(("parallel",)),
        name="rw_post",
    )(y, bonus, proj, proj, proj, proj, rw["gn_w"], rw["gn_b"], rw["seg"])


def _block_diag(w):
    nb, d, _ = w.shape
    eye = jnp.eye(nb, dtype=w.dtype)
    return (eye[:, None, :, None] * w[:, :, None, :]).reshape(nb * d, nb * d)


def _layer_tail(x, proj, osb, olru, orw, p, lp, fg, final, nseq):
    h = _outproj(x, osb, olru, orw, p, lp["w_out"], lp["ple_gate"], lp["ple_proj"], fg,
                 layer=lp["layer"], tm=lp["tm_out"], final=final)
    t = proj.shape[0] // nseq
    p3 = proj.reshape(nseq, t, D_IN_PROJ)
    hist = CONV_WIDTH - 1
    return h, (proj, proj, p3[:, t - hist:, OFF_LRU_X:OFF_LRU_G], p3[:, t - 1, OFF_RW:OFF_RW_G])


def _kv_new_body(*refs, depth):
    ins, (ko_ref, vo_ref) = refs[:2 * depth], refs[2 * depth:]
    rows = ins[0].shape[0]
    for l in range(depth):
        for src, dst in ((ins[2 * l], ko_ref), (ins[2 * l + 1], vo_ref)):
            x = src[...]
            for h in range(SB_HEADS):
                dst[l, pl.ds(h, rows, stride=SB_HEADS), :] = x[:, h * SB_HEAD_DIM:(h + 1) * SB_HEAD_DIM]


def _kv_new(projs, *, tb):
    depth = len(projs)
    rows = projs[0].shape[0]
    col = lambda off: pl.BlockSpec((tb, SB_WIDTH), lambda i: (i, off // SB_WIDTH))
    out_spec = pl.BlockSpec((depth, tb * SB_HEADS, SB_HEAD_DIM), lambda i: (0, i, 0))
    out_shape = jax.ShapeDtypeStruct((depth, rows * SB_HEADS, SB_HEAD_DIM), F32)
    return pl.pallas_call(
        functools.partial(_kv_new_body, depth=depth),
        grid=(rows // tb,),
        in_specs=[col(OFF_SB_K), col(OFF_SB_V)] * depth,
        out_specs=[out_spec, out_spec],
        out_shape=[out_shape, out_shape],
        compiler_params=_cparams(("parallel",)),
        name="kv_new",
    )(*[p for proj in projs for p in (proj, proj)])


def _layer_prompt(x, p, lp, fg, *, batch, seq, final):
    proj = _inproj(x, lp["norm_g"], lp["w_in"], layer=lp["layer"], tm=lp["tm_in"], tn=lp["tn_in"])
    osb = _attn_prompt(proj, batch=batch, seq=seq, tq=lp["tq"])
    olru, h_last = _lru_prompt(proj, lp["lru"], batch=batch, seq=seq, tb=lp["lru_tb"])
    orw, s_new = _rw_prompt(proj, lp["rw"], batch=batch, seq=seq, tb=lp["rw_tb"])
    h, (k_new, v_new, conv_new, shift_new) = _layer_tail(x, proj, osb, olru, orw, p, lp, fg, final, batch)
    return h, (k_new, v_new, h_last.reshape(batch, LRU_WIDTH), conv_new, s_new, shift_new)


def _layer_sample(x, p, cache_k, cache_v, page_table, lru_conv, lru_h, rw_shift, rw_s, lp, fg, *,
                  layer, dec_batch, t_new, final):
    proj = _inproj(x, lp["norm_g"], lp["w_in"], layer=lp["layer"], tm=lp["tm_in"], tn=lp["tn_in"])
    osb = _attn_sample(proj, cache_k, cache_v, page_table, layer=layer, dec_batch=dec_batch, t_new=t_new,
                       bb=lp["attn_bb"])
    olru, h_last = _lru_sample(proj, lru_conv, lru_h, lp["lru"], nb=dec_batch, t_new=t_new)
    pre = _rw_pre_sample(proj, rw_shift, lp["rw"], nb=dec_batch, t_new=t_new)
    y_rw, s_new = _rw_scan_sample(pre[:6], rw_s, nb=dec_batch, t_new=t_new)
    orw = _rw_post(y_rw, pre[6], proj, lp["rw"], tb=lp["tm_in"])
    h, (k_new, v_new, conv_new, shift_new) = _layer_tail(x, proj, osb, olru, orw, p, lp, fg, final,
                                                          dec_batch)
    return h, (k_new, v_new, h_last, conv_new, s_new, shift_new)


def _layer_params(l, norm_g, w_in, w_out, lru_conv_w, lru_conv_b, lru_gate_a_w, lru_gate_a_b,
                  lru_gate_x_w, lru_gate_x_b, lru_lambda, rw_mu, rw_w0, rw_w2, rw_a0, rw_a2,
                  rw_k_k, rw_k_a, rw_r_k, rw_gn_w, rw_gn_b, ple_proj, ple_gate):
    row = lambda a: a[l].reshape(1, -1)
    w = RW_WIDTH
    zeros = jnp.zeros((RW_LORA, w), F32)
    head_id = jnp.arange(w) // RW_HEAD_DIM
    lru = dict(conv_w=lru_conv_w[l], conv_b=row(lru_conv_b),
               gate_w=jnp.concatenate([_block_diag(lru_gate_a_w[l]), _block_diag(lru_gate_x_w[l])],
                                      axis=1).astype(BF16),
               gate_a_b=row(lru_gate_a_b), gate_x_b=row(lru_gate_x_b), lam=row(lru_lambda))
    rw = dict(mu=row(rw_mu), w0=row(rw_w0),
              w2=jnp.concatenate([rw_w2[l], zeros], axis=0).astype(BF16),
              a0=row(rw_a0),
              a2=jnp.concatenate([zeros, rw_a2[l]], axis=0).astype(BF16),
              k_k=row(rw_k_k), k_a=row(rw_k_a), r_k=row(rw_r_k),
              gn_w=row(rw_gn_w), gn_b=row(rw_gn_b),
              seg=(head_id[:, None] == head_id[None, :]).astype(BF16))
    return dict(layer=l, norm_g=row(norm_g), w_in=w_in, w_out=w_out, ple_proj=ple_proj, ple_gate=ple_gate,
                lru=lru, rw=rw)


def _tiles(seq):
    return dict(tm_in=512, tn_in=D_IN_PROJ // 3, tq=128, attn_bb=4, lru_tb=min(1024, seq),
                rw_tb=min(1024, seq), tm_out=256)


def kernel(x_prompt, x_sample, p_prompt, p_sample, cache_sb_k, cache_sb_v, page_table, state_lru_h, state_lru_conv, state_rw_S, state_rw_shift, norm_g, w_in, w_out, lru_conv_w, lru_conv_b, lru_gate_a_w, lru_gate_a_b, lru_gate_x_w, lru_gate_x_b, lru_lambda, rw_mu, rw_w0, rw_w2, rw_a0, rw_a2, rw_k_k, rw_k_a, rw_r_k, rw_gn_w, rw_gn_b, ple_proj, ple_gate, final_norm_g):
    batch, seq, d = x_prompt.shape
    dec_batch, t_new, _ = x_sample.shape
    depth = w_in.shape[0]
    n_p, n_s = batch * seq, dec_batch * t_new
    xp = x_prompt.reshape(n_p, d)
    xs = x_sample.reshape(n_s, d)
    fg = final_norm_g.reshape(1, d)
    w_in, w_out, ple_proj, ple_gate = (a.astype(BF16) for a in (w_in, w_out, ple_proj, ple_gate))
    pp = p_prompt.reshape(depth, n_p, -1)
    ps = p_sample.reshape(depth, n_s, -1)
    outs_p, outs_s = [], []
    for l in range(depth):
        lp = _layer_params(l, norm_g, w_in, w_out, lru_conv_w, lru_conv_b, lru_gate_a_w, lru_gate_a_b,
                           lru_gate_x_w, lru_gate_x_b, lru_lambda, rw_mu, rw_w0, rw_w2, rw_a0, rw_a2,
                           rw_k_k, rw_k_a, rw_r_k, rw_gn_w, rw_gn_b, ple_proj, ple_gate)
        lp.update(_tiles(seq))
        final = l == depth - 1
        xp, st_p = _layer_prompt(xp, pp, lp, fg, batch=batch, seq=seq, final=final)
        xs, st_s = _layer_sample(xs, ps, cache_sb_k, cache_sb_v, page_table,
                                 state_lru_conv[l], state_lru_h[l], state_rw_shift[l], state_rw_S[l], lp, fg,
                                 layer=l, dec_batch=dec_batch, t_new=t_new, final=final)
        outs_p.append(st_p)
        outs_s.append(st_s)
    y_prompt = xp.reshape(batch, seq, d)
    y_sample = xs.reshape(dec_batch, t_new, d)
    stack = lambda outs, k: jnp.stack([o[k] for o in outs])
    tiles = _tiles(seq)
    k_p, v_p = _kv_new([o[0] for o in outs_p], tb=tiles["tm_in"])
    k_s, v_s = _kv_new([o[0] for o in outs_s], tb=tiles["tm_in"])
    heads_p = lambda a: a.reshape(depth, batch, seq, SB_HEADS, SB_HEAD_DIM)
    heads_s = lambda a: a.reshape(depth, dec_batch, t_new, SB_HEADS, SB_HEAD_DIM)
    return (y_prompt, y_sample, heads_p(k_p), heads_p(v_p), heads_s(k_s), heads_s(v_s),
            stack(outs_p, 2), stack(outs_s, 2), stack(outs_p, 3), stack(outs_s, 3),
            stack(outs_p, 4), stack(outs_s, 4), stack(outs_p, 5), stack(outs_s, 5))
```
